```python
import math
import jax, jax.numpy as jnp
from jax import lax
import numpy as np

D_MODEL = 4096
BATCH = 8
SEQ = 2048
DEPTH = 2
DEC_BATCH = 2
DEC_SEQ = 8192
PAST_LEN = 128

N_META = 16
N_MIXERS = 2
RET_HEADS = 16
RET_DK = D_MODEL // RET_HEADS
RET_DV = 2 * RET_DK
RET_QK = RET_HEADS * RET_DK
RET_V = RET_HEADS * RET_DV
RET_CHUNK = 128
RET_THETA = 10000.0
DIFF_HEADS = 32
DIFF_DH = 128
DIFF_DV = 2 * DIFF_DH
DIFF_QK = DIFF_HEADS * 2 * DIFF_DH
DIFF_V = DIFF_HEADS * DIFF_DV
DIFF_ROT = DIFF_DH // 4
ROPE_THETA = 500000.0
Q_BLOCK = 128
N_RET = (DEPTH + 1) // 2
N_DIFF = DEPTH // 2
NORM_EPS = 1e-6

kernel_name = 'hybrid_retention_diffattn_encoder'


def rms_norm(x, gain=None, eps=NORM_EPS):
    xf = x.astype(jnp.float32)
    y = xf * lax.rsqrt(jnp.mean(xf * xf, axis=-1, keepdims=True) + eps)
    if gain is not None:
        y = y * gain.astype(jnp.float32)
    return y.astype(x.dtype)


def rotary(x, pos, rot_dim, theta):
    half = rot_dim // 2
    inv_freq = jnp.power(jnp.float32(theta), -jnp.arange(half, dtype=jnp.float32) * 2.0 / rot_dim)
    ang = pos[:, None] * inv_freq[None, :]
    shape = (pos.shape[0],) + (1,) * (x.ndim - 3) + (half,)
    cos = jnp.cos(ang).reshape(shape)
    sin = jnp.sin(ang).reshape(shape)
    xf = x.astype(jnp.float32)
    x1 = xf[..., :half]
    x2 = xf[..., half:rot_dim]
    out = jnp.concatenate([x1 * cos - x2 * sin, x2 * cos + x1 * sin, xf[..., rot_dim:]], axis=-1)
    return out.astype(x.dtype)


def retention_chunkwise(q, k, v, log_gamma, include_diag):
    B, Lp, H, dk = q.shape
    dv = v.shape[-1]
    C = RET_CHUNK
    N = Lp // C
    qc = q.reshape(B, N, C, H, dk)
    kc = k.reshape(B, N, C, H, dk)
    vc = v.reshape(B, N, C, H, dv)
    idx = jnp.arange(C, dtype=jnp.float32)
    rel = idx[:, None] - idx[None, :]
    mask = (rel >= 0) if include_diag else (rel > 0)
    decay = jnp.where(mask[None], jnp.exp(jnp.where(mask, rel, 0.0)[None] * log_gamma[:, None, None]), 0.0)
    scores = jnp.einsum('bnihd,bnjhd->bnhij', qc, kc) * decay[None, None]
    intra = jnp.einsum('bnhij,bnjhe->bnihe', scores, vc)
    q_decay = jnp.exp((idx + 1.0)[:, None] * log_gamma[None, :])
    k_decay = jnp.exp((C - 1.0 - idx)[:, None] * log_gamma[None, :])
    chunk_decay = jnp.exp(C * log_gamma)

    def step(state, inp):
        qn, kn, vn = inp
        cross = jnp.einsum('bihd,bhde->bihe', qn * q_decay[None, :, :, None], state)
        state = state * chunk_decay[None, :, None, None] + jnp.einsum(
            'bjhd,bjhe->bhde', kn * k_decay[None, :, :, None], vn)
        return state, cross

    init = jnp.zeros((B, H, dk, dv), jnp.float32)
    _, cross = lax.scan(step, init, (jnp.moveaxis(qc, 1, 0), jnp.moveaxis(kc, 1, 0), jnp.moveaxis(vc, 1, 0)))
    out = intra + jnp.moveaxis(cross, 0, 1)
    return out.reshape(B, Lp, H, dv)


def retention_mixer(h, w_in, w_out, decay_fwd_raw, decay_bwd_raw):
    B, L, _ = h.shape
    proj = h @ w_in
    q, k, v, g = jnp.split(proj, [RET_QK, 2 * RET_QK, 2 * RET_QK + RET_V], axis=-1)
    q = q.reshape(B, L, RET_HEADS, RET_DK)
    k = k.reshape(B, L, RET_HEADS, RET_DK)
    v = v.reshape(B, L, RET_HEADS, RET_DV)
    pos = jnp.arange(L, dtype=jnp.float32)
    q = rotary(q, pos, RET_DK, RET_THETA).astype(jnp.float32) * (RET_DK ** -0.5)
    k = rotary(k, pos, RET_DK, RET_THETA).astype(jnp.float32)
    pad = RET_CHUNK - N_META
    padt = lambda t: jnp.pad(t.astype(jnp.float32), ((0, 0), (pad, 0), (0, 0), (0, 0)))
    qp, kp, vp = padt(q), padt(k), padt(v)
    lg_f = -jnp.exp(decay_fwd_raw.astype(jnp.float32))
    lg_b = -jnp.exp(decay_bwd_raw.astype(jnp.float32))
    fwd = retention_chunkwise(qp, kp, vp, lg_f, True)
    flip = lambda t: jnp.flip(t, axis=1)
    bwd = flip(retention_chunkwise(flip(qp), flip(kp), flip(vp), lg_b, False))
    o = (fwd + bwd)[:, pad:]
    o = rms_norm(o)
    o = o.reshape(B, L, RET_V).astype(h.dtype) * jax.nn.silu(g)
    return o @ w_out


def diff_attn_mixer(h, w_in, w_out, lq1, lk1, lq2, lk2, subln, lambda_init):
    B, L, _ = h.shape
    proj = h @ w_in
    q, k, v, g = jnp.split(proj, [DIFF_QK, 2 * DIFF_QK, 2 * DIFF_QK + DIFF_V], axis=-1)
    q = q.reshape(B, L, DIFF_HEADS, 2, DIFF_DH)
    k = k.reshape(B, L, DIFF_HEADS, 2, DIFF_DH)
    v = v.reshape(B, L, DIFF_HEADS, DIFF_DV)
    pos = jnp.arange(L, dtype=jnp.float32)
    q = rotary(q, pos, DIFF_ROT, ROPE_THETA)
    k = rotary(k, pos, DIFF_ROT, ROPE_THETA)
    lam = (jnp.exp(jnp.sum(lq1.astype(jnp.float32) * lk1.astype(jnp.float32)))
           - jnp.exp(jnp.sum(lq2.astype(jnp.float32) * lk2.astype(jnp.float32))) + lambda_init)
    nb = -(-L // Q_BLOCK)
    Lq = nb * Q_BLOCK
    qb = jnp.pad(q, ((0, 0), (0, Lq - L), (0, 0), (0, 0), (0, 0))).reshape(B, nb, Q_BLOCK, DIFF_HEADS, 2, DIFF_DH)
    qb = jnp.moveaxis(qb, 1, 0)
    scale = DIFF_DH ** -0.5

    def block(qi):
        s = jnp.einsum('bqhmd,bkhmd->bhmqk', qi, k).astype(jnp.float32) * scale
        p = jax.nn.softmax(s, axis=-1)
        a = p[:, :, 0] - lam * p[:, :, 1]
        return jnp.einsum('bhqk,bkhe->bqhe', a.astype(v.dtype), v)

    o = lax.map(block, qb)
    o = jnp.moveaxis(o, 0, 1).reshape(B, Lq, DIFF_HEADS, DIFF_DV)[:, :L]
    o = rms_norm(o, subln, eps=1e-5) * (1.0 - lambda_init)
    o = o.reshape(B, L, DIFF_V) * jax.nn.silu(g)
    return o @ w_out


def setup_inputs(seed: int = 0) -> dict:
    key = jax.random.key(seed)
    ks = jax.random.split(key, 16)
    f32 = jnp.float32
    ret_in = 2 * RET_QK + 2 * RET_V
    diff_in = 2 * DIFF_QK + 2 * DIFF_V
    base = (-5.0 - jnp.arange(RET_HEADS, dtype=f32)) * math.log(2.0)
    return {
        'x_prompt': jax.random.normal(ks[0], (BATCH, SEQ, D_MODEL), f32),
        'x_sample': jax.random.normal(ks[1], (DEC_BATCH, DEC_SEQ, D_MODEL), f32),
        'meta_tokens': jax.random.normal(ks[2], (N_META, D_MODEL), f32),
        'pre_norm': 1.0 + 0.02 * jax.random.normal(ks[3], (DEPTH, D_MODEL), f32),
        'post_norm': 1.0 + 0.02 * jax.random.normal(ks[4], (DEPTH, D_MODEL), f32),
        'ret_w_in': jax.random.normal(ks[5], (N_RET, D_MODEL, ret_in), f32) * D_MODEL ** -0.5,
        'ret_w_out': jax.random.normal(ks[6], (N_RET, RET_V, D_MODEL), f32) * RET_V ** -0.5,
        'ret_decay_fwd': base[None, :] + 0.05 * jax.random.normal(ks[7], (N_RET, RET_HEADS), f32),
        'ret_decay_bwd': base[None, :] + 0.05 * jax.random.normal(ks[8], (N_RET, RET_HEADS), f32),
        'diff_w_in': jax.random.normal(ks[9], (N_DIFF, D_MODEL, diff_in), f32) * D_MODEL ** -0.5,
        'diff_w_out': jax.random.normal(ks[10], (N_DIFF, DIFF_V, D_MODEL), f32) * DIFF_V ** -0.5,
        'diff_lambda_q1': 0.1 * jax.random.normal(ks[11], (N_DIFF, DIFF_DH), f32),
        'diff_lambda_k1': 0.1 * jax.random.normal(ks[12], (N_DIFF, DIFF_DH), f32),
        'diff_lambda_q2': 0.1 * jax.random.normal(ks[13], (N_DIFF, DIFF_DH), f32),
        'diff_lambda_k2': 0.1 * jax.random.normal(ks[14], (N_DIFF, DIFF_DH), f32),
        'diff_subln': 1.0 + 0.02 * jax.random.normal(ks[15], (N_DIFF, DIFF_DV), f32),
    }


def reference(x_prompt, x_sample, meta_tokens, pre_norm, post_norm, ret_w_in, ret_w_out,
              ret_decay_fwd, ret_decay_bwd, diff_w_in, diff_w_out, diff_lambda_q1,
              diff_lambda_k1, diff_lambda_q2, diff_lambda_k2, diff_subln):
    def trunk(x):
        B = x.shape[0]
        meta = jnp.broadcast_to(meta_tokens.astype(x.dtype)[None], (B, N_META, D_MODEL))
        x = jnp.concatenate([meta, x], axis=1)
        for i in range(DEPTH):
            h = rms_norm(x, pre_norm[i])
            j = i // N_MIXERS
            if i % N_MIXERS == 0:
                m = retention_mixer(h, ret_w_in[j], ret_w_out[j], ret_decay_fwd[j], ret_decay_bwd[j])
            else:
                lambda_init = 0.8 - 0.6 * math.exp(-0.3 * i)
                m = diff_attn_mixer(h, diff_w_in[j], diff_w_out[j], diff_lambda_q1[j], diff_lambda_k1[j],
                                    diff_lambda_q2[j], diff_lambda_k2[j], diff_subln[j], lambda_init)
            x = x + rms_norm(m, post_norm[i])
        return x[:, N_META:]

    y_prompt = trunk(x_prompt)
    y_sample = trunk(x_sample)
    return (y_prompt, y_sample)
```

```python
import functools
import math

import jax
import jax.numpy as jnp
from jax import lax
from jax.experimental import pallas as pl
from jax.experimental.pallas import tpu as pltpu

N_META = 16
N_MIXERS = 2
RET_HEADS = 16
RET_CHUNK = 128
RET_THETA = 10000.0
DIFF_HEADS = 32
DIFF_DH = 128
DIFF_DV = 2 * DIFF_DH
DIFF_ROT = DIFF_DH // 4
ROPE_THETA = 500000.0
NORM_EPS = 1e-6
SUBLN_EPS = 1e-5

ROW_ALIGN = 128
PAD_ROWS = ROW_ALIGN - N_META
MASK_BIAS = -1e30
V7X_VMEM_LIMIT_BYTES = 56 * 1024 * 1024

MM_BLOCK_M = 1024
MM_BLOCK_N = 1024
RESNORM_BLOCK_M = 256
ATTN_BLOCK_Q = 1024
ATTN_BLOCK_K = 2176
RET_CHUNKS_PER_STEP = 17

F32 = jnp.float32
BF16 = jnp.bfloat16


def _divisor_block(total, target, mult):
    best = None
    for cand in range(mult, min(total, target) + 1, mult):
        if total % cand == 0:
            best = cand
    assert best is not None, (total, target, mult)
    return best


def _params(*sem):
    return pltpu.CompilerParams(dimension_semantics=sem, vmem_limit_bytes=V7X_VMEM_LIMIT_BYTES)


def _rms(xf, gain, eps):
    y = xf * lax.rsqrt(jnp.mean(xf * xf, axis=-1, keepdims=True) + eps)
    if gain is not None:
        y = y * gain
    return y


def _silu(g):
    return g * (1.0 / (1.0 + jnp.exp(-g)))


def _embed_norm_kernel(x_ref, meta_ref, gain_ref, x0_ref, h_ref):
    t = pl.program_id(1)
    xb = jnp.where(t == 0, meta_ref[...], x_ref[0])
    x0_ref[0] = xb
    h_ref[0] = _rms(xb, gain_ref[...], NORM_EPS).astype(BF16)


def _embed_norm(x, meta_block, gain):
    B, S, D = x.shape
    n_blk = S // ROW_ALIGN + 1
    Lp = n_blk * ROW_ALIGN
    blk = (1, ROW_ALIGN, D)
    return pl.pallas_call(
        _embed_norm_kernel,
        grid=(B, n_blk),
        in_specs=[
            pl.BlockSpec(blk, lambda b, t: (b, jnp.maximum(t - 1, 0), 0)),
            pl.BlockSpec((ROW_ALIGN, D), lambda b, t: (0, 0)),
            pl.BlockSpec((1, D), lambda b, t: (0, 0)),
        ],
        out_specs=[pl.BlockSpec(blk, lambda b, t: (b, t, 0)),
                   pl.BlockSpec(blk, lambda b, t: (b, t, 0))],
        out_shape=[jax.ShapeDtypeStruct((B, Lp, D), F32),
                   jax.ShapeDtypeStruct((B, Lp, D), BF16)],
        compiler_params=_params("arbitrary", "arbitrary"),
        name="embed_norm",
    )(x, meta_block, gain)


def _mm_plain_kernel(x_ref, w_ref, o_ref):
    o_ref[...] = jnp.dot(x_ref[...], w_ref[...], preferred_element_type=F32).astype(o_ref.dtype)


def _mm_rot_full_kernel(x_ref, w_ref, cos_ref, sin_ref, o_ref, *, head_dim, n_scaled_blocks, scale):
    acc = jnp.dot(x_ref[...], w_ref[...], preferred_element_type=F32)
    sc = jnp.where(pl.program_id(1) < n_scaled_blocks, scale, 1.0).astype(F32)
    c = cos_ref[...] * sc
    s = sin_ref[...] * sc
    half = head_dim // 2
    for h0 in range(0, acc.shape[1], head_dim):
        x1 = acc[:, h0:h0 + half]
        x2 = acc[:, h0 + half:h0 + head_dim]
        o_ref[:, h0:h0 + half] = (x1 * c - x2 * s).astype(o_ref.dtype)
        o_ref[:, h0 + half:h0 + head_dim] = (x2 * c + x1 * s).astype(o_ref.dtype)


def _mm_rot_part_kernel(x_ref, w_ref, c_ref, s_up_ref, s_dn_ref, o_ref, *, group, rot_half,
                        n_scaled_blocks, scale):
    acc = jnp.dot(x_ref[...], w_ref[...], preferred_element_type=F32)
    sc = jnp.where(pl.program_id(1) < n_scaled_blocks, scale, 1.0).astype(F32)
    c = c_ref[...] * sc
    s_up = s_up_ref[...] * sc
    s_dn = s_dn_ref[...] * sc
    for g0 in range(0, acc.shape[1], group):
        xg = acc[:, g0:g0 + group]
        from_lo = pltpu.roll(xg, rot_half, 1)
        from_hi = pltpu.roll(xg, group - rot_half, 1)
        o_ref[:, g0:g0 + group] = (xg * c + from_lo * s_up + from_hi * s_dn).astype(o_ref.dtype)


def _matmul(x, w, *, bm, bn, out_dtype, body=_mm_plain_kernel, row_tables=(), name):
    M, K = x.shape
    N = w.shape[1]
    tab_specs = [pl.BlockSpec((bm, t.shape[1]), lambda i, j: (i, 0)) for t in row_tables]
    return pl.pallas_call(
        body,
        grid=(M // bm, N // bn),
        in_specs=[pl.BlockSpec((bm, K), lambda i, j: (i, 0)),
                  pl.BlockSpec((K, bn), lambda i, j: (0, j))] + tab_specs,
        out_specs=pl.BlockSpec((bm, bn), lambda i, j: (i, j)),
        out_shape=jax.ShapeDtypeStruct((M, N), out_dtype),
        compiler_params=_params("arbitrary", "arbitrary"),
        name=name,
    )(x, w, *row_tables)


def _mm_acc_kernel(x_ref, w_ref, o_ref):
    part = jnp.dot(x_ref[...], w_ref[...], preferred_element_type=F32)
    k = pl.program_id(2)

    @pl.when(k == 0)
    def _():
        o_ref[...] = part

    @pl.when(k != 0)
    def _():
        o_ref[...] += part


def _matmul_ksplit(x, w, *, bm, bn, bk, name):
    M, K = x.shape
    N = w.shape[1]
    return pl.pallas_call(
        _mm_acc_kernel,
        grid=(M // bm, N // bn, K // bk),
        in_specs=[pl.BlockSpec((bm, bk), lambda i, j, k: (i, k)),
                  pl.BlockSpec((bk, bn), lambda i, j, k: (k, j))],
        out_specs=pl.BlockSpec((bm, bn), lambda i, j, k: (i, j)),
        out_shape=jax.ShapeDtypeStruct((M, N), F32),
        compiler_params=_params("arbitrary", "arbitrary", "arbitrary"),
        name=name,
    )(x, w)


def _retention_kernel(rawf_ref, rawb_ref, q_ref, k_ref, v_ref, g_ref, o_ref,
                      state_ref, stash_ref, dmat_ref, qdec_ref, kdec_ref, cdec_ref,
                      *, chunks_per_step, n_steps):
    C = RET_CHUNK
    T = chunks_per_step
    phase = pl.program_id(2)
    step = pl.program_id(3)
    dk = q_ref.shape[1]
    dv = v_ref.shape[1]

    @pl.when((phase == 0) & (step == 0))
    def _():
        lgf = jnp.broadcast_to(-jnp.exp(rawf_ref[0])[0:1, :], (C, C))
        lgb = jnp.broadcast_to(-jnp.exp(rawb_ref[0])[0:1, :], (C, C))
        row = lax.broadcasted_iota(jnp.int32, (C, C), 0).astype(F32)
        col = lax.broadcasted_iota(jnp.int32, (C, C), 1).astype(F32)
        rel = row - col
        d_f = jnp.where(rel >= 0, jnp.exp(jnp.where(rel >= 0, rel, 0.0) * lgf), 0.0)
        d_b = jnp.where(rel < 0, jnp.exp(jnp.where(rel < 0, -rel, 0.0) * lgb), 0.0)
        dmat_ref[...] = d_f + d_b
        reps_k = dk // C
        qdec_ref[0] = jnp.concatenate([jnp.exp((row + 1.0) * lgf)] * reps_k, axis=1)
        kdec_ref[0] = jnp.concatenate([jnp.exp((C - 1.0 - row) * lgf)] * reps_k, axis=1)
        qdec_ref[1] = jnp.concatenate([jnp.exp((C - row) * lgb)] * reps_k, axis=1)
        kdec_ref[1] = jnp.concatenate([jnp.exp(row * lgb)] * reps_k, axis=1)
        reps_v = dv // C
        cdec_ref[0] = jnp.concatenate([jnp.exp(C * lgf[0:8, :])] * reps_v, axis=1)
        cdec_ref[1] = jnp.concatenate([jnp.exp(C * lgb[0:8, :])] * reps_v, axis=1)

    @pl.when(step == 0)
    def _():
        state_ref[...] = jnp.zeros_like(state_ref)

    def cross_and_update(direction, rows):
        qc = q_ref[rows, :]
        kc = k_ref[rows, :]
        vc = v_ref[rows, :]
        st = state_ref[...]
        qd = (qc.astype(F32) * qdec_ref[direction]).astype(BF16)
        cross = jnp.dot(qd, st.astype(BF16), preferred_element_type=F32)
        kd = (kc.astype(F32) * kdec_ref[direction]).astype(BF16)
        upd = lax.dot_general(kd, vc, (((0,), (0,)), ((), ())), preferred_element_type=F32)
        state_ref[...] = st * cdec_ref[direction][0:1, :] + upd
        return qc, kc, vc, cross

    @pl.when(phase == 0)
    def _():
        for c in range(T):
            rows = pl.ds(c * C, C)
            qc, kc, vc, cross = cross_and_update(0, rows)
            a = lax.dot_general(qc, kc, (((1,), (1,)), ((), ())), preferred_element_type=F32)
            pm = (a * dmat_ref[...]).astype(BF16)
            intra = jnp.dot(pm, vc, preferred_element_type=F32)
            stash_ref[step * T + c] = intra + cross

    @pl.when(phase == 1)
    def _():
        blk = n_steps - 1 - step
        for c in range(T - 1, -1, -1):
            rows = pl.ds(c * C, C)
            _, _, _, cross = cross_and_update(1, rows)
            o = stash_ref[blk * T + c] + cross
            y = _rms(o, None, NORM_EPS)
            y = y * _silu(g_ref[rows, :].astype(F32))
            o_ref[rows, :] = y.astype(o_ref.dtype)


def _retention(qk, vg, raw_f, raw_b, *, batch, n_chunks, chunks_per_step):
    H = RET_HEADS
    R = qk.shape[0]
    dk = qk.shape[1] // (2 * H)
    dv = vg.shape[1] // (2 * H)
    T = chunks_per_step
    NS = n_chunks // T
    rows = T * RET_CHUNK

    def sweep_block(b, p, s):
        return b * NS + jnp.where(p == 0, s, NS - 1 - s)

    def store_block(b, p, s):
        return b * NS + jnp.where(p == 0, NS - 1, NS - 1 - s)

    kern = functools.partial(_retention_kernel, chunks_per_step=T, n_steps=NS)
    return pl.pallas_call(
        kern,
        grid=(batch, H, 2, NS),
        in_specs=[
            pl.BlockSpec((1, 8, 128), lambda b, h, p, s: (h, 0, 0)),
            pl.BlockSpec((1, 8, 128), lambda b, h, p, s: (h, 0, 0)),
            pl.BlockSpec((rows, dk), lambda b, h, p, s: (sweep_block(b, p, s), h)),
            pl.BlockSpec((rows, dk), lambda b, h, p, s: (sweep_block(b, p, s), H + h)),
            pl.BlockSpec((rows, dv), lambda b, h, p, s: (sweep_block(b, p, s), h)),
            pl.BlockSpec((rows, dv), lambda b, h, p, s: (store_block(b, p, s), H + h)),
        ],
        out_specs=pl.BlockSpec((rows, dv), lambda b, h, p, s: (store_block(b, p, s), h)),
        out_shape=jax.ShapeDtypeStruct((R, H * dv), BF16),
        scratch_shapes=[
            pltpu.VMEM((dk, dv), F32),
            pltpu.VMEM((n_chunks, RET_CHUNK, dv), F32),
            pltpu.VMEM((RET_CHUNK, RET_CHUNK), F32),
            pltpu.VMEM((2, RET_CHUNK, dk), F32),
            pltpu.VMEM((2, RET_CHUNK, dk), F32),
            pltpu.VMEM((2, 8, dv), F32),
        ],
        compiler_params=_params("arbitrary", "arbitrary", "arbitrary", "arbitrary"),
        name="retention",
    )(raw_f, raw_b, qk, qk, vg, vg)


def _resnorm_kernel(m_ref, x_ref, post_ref, pre_ref, xo_ref, h_ref):
    xn = x_ref[...] + _rms(m_ref[...], post_ref[...], NORM_EPS)
    xo_ref[...] = xn
    h_ref[...] = _rms(xn, pre_ref[...], NORM_EPS).astype(BF16)


def _resnorm(m, x, post, pre, *, bm):
    R, D = m.shape
    row = pl.BlockSpec((bm, D), lambda i: (i, 0))
    vec = pl.BlockSpec((1, D), lambda i: (0, 0))
    return pl.pallas_call(
        _resnorm_kernel,
        grid=(R // bm,),
        in_specs=[row, row, vec, vec],
        out_specs=[row, row],
        out_shape=[jax.ShapeDtypeStruct((R, D), F32), jax.ShapeDtypeStruct((R, D), BF16)],
        compiler_params=_params("arbitrary"),
        name="resnorm",
    )(m, x, post, pre)


def _resnorm_final_kernel(m_ref, x_ref, post_ref, y_ref):
    y_ref[...] = x_ref[...] + _rms(m_ref[...], post_ref[...], NORM_EPS)


def _resnorm_final(m, x, post, *, batch):
    R, D = m.shape
    Lp = R // batch
    n_blk = Lp // ROW_ALIGN
    m3 = m.reshape(batch, Lp, D)
    x3 = x.reshape(batch, Lp, D)
    blk = (1, ROW_ALIGN, D)
    return pl.pallas_call(
        _resnorm_final_kernel,
        grid=(batch, n_blk - 1),
        in_specs=[pl.BlockSpec(blk, lambda b, t: (b, t + 1, 0)),
                  pl.BlockSpec(blk, lambda b, t: (b, t + 1, 0)),
                  pl.BlockSpec((1, 1, D), lambda b, t: (0, 0, 0))],
        out_specs=pl.BlockSpec(blk, lambda b, t: (b, t, 0)),
        out_shape=jax.ShapeDtypeStruct((batch, Lp - ROW_ALIGN, D), F32),
        compiler_params=_params("arbitrary", "arbitrary"),
        name="resnorm_final",
    )(m3, x3, post.reshape(1, 1, D))


def _diff_attn_kernel(lq1_ref, lk1_ref, lq2_ref, lk2_ref, subln_ref, bias_ref,
                      q_ref, k_ref, v_ref, g_ref, o_ref, m_ref, l_ref, acc_ref,
                      *, lambda_init, n_kv):
    kv = pl.program_id(3)
    dh = DIFF_DH

    @pl.when(kv == 0)
    def _():
        m_ref[...] = jnp.full_like(m_ref, -jnp.inf)
        l_ref[...] = jnp.zeros_like(l_ref)
        acc_ref[...] = jnp.zeros_like(acc_ref)

    bias = bias_ref[...]
    v = v_ref[...]
    for mi in range(2):
        qm = q_ref[:, mi * dh:(mi + 1) * dh]
        km = k_ref[:, mi * dh:(mi + 1) * dh]
        s = lax.dot_general(qm, km, (((1,), (1,)), ((), ())), preferred_element_type=F32) + bias
        m_prev = m_ref[mi]
        m_new = jnp.maximum(m_prev, jnp.max(s, axis=1, keepdims=True))
        alpha = jnp.exp(m_prev - m_new)
        p = jnp.exp(s - m_new[:, 0:1])
        l_ref[mi] = alpha * l_ref[mi] + jnp.sum(p, axis=1, keepdims=True)
        pv = jnp.dot(p.astype(BF16), v, preferred_element_type=F32)
        acc_ref[mi] = acc_ref[mi] * alpha[:, 0:1] + pv
        m_ref[mi] = m_new

    @pl.when(kv == n_kv - 1)
    def _():
        lam = (jnp.exp(jnp.sum(lq1_ref[...] * lk1_ref[...], axis=1, keepdims=True))
               - jnp.exp(jnp.sum(lq2_ref[...] * lk2_ref[...], axis=1, keepdims=True))
               + lambda_init)
        o1 = acc_ref[0] / l_ref[0][:, 0:1]
        o2 = acc_ref[1] / l_ref[1][:, 0:1]
        o = o1 - lam * o2
        y = _rms(o, subln_ref[...], SUBLN_EPS) * (1.0 - lambda_init)
        y = y * _silu(g_ref[...].astype(F32))
        o_ref[...] = y.astype(o_ref.dtype)


def _diff_attention(qk, vg, lam_params, subln, bias, *, batch, lambda_init, tq, tk):
    H = DIFF_HEADS
    R = qk.shape[0]
    Lp = R // batch
    nq = Lp // tq
    nk = Lp // tk
    w = 2 * DIFF_DH
    vec = pl.BlockSpec((1, DIFF_DH), lambda b, h, i, j: (0, 0))
    kern = functools.partial(_diff_attn_kernel, lambda_init=lambda_init, n_kv=nk)
    return pl.pallas_call(
        kern,
        grid=(batch, H, nq, nk),
        in_specs=[
            vec, vec, vec, vec,
            pl.BlockSpec((1, DIFF_DV), lambda b, h, i, j: (0, 0)),
            pl.BlockSpec((1, tk), lambda b, h, i, j: (0, j)),
            pl.BlockSpec((tq, w), lambda b, h, i, j: (b * nq + i, h)),
            pl.BlockSpec((tk, w), lambda b, h, i, j: (b * nk + j, H + h)),
            pl.BlockSpec((tk, DIFF_DV), lambda b, h, i, j: (b * nk + j, h)),
            pl.BlockSpec((tq, DIFF_DV), lambda b, h, i, j: (b * nq + i, H + h)),
        ],
        out_specs=pl.BlockSpec((tq, DIFF_DV), lambda b, h, i, j: (b * nq + i, h)),
        out_shape=jax.ShapeDtypeStruct((R, H * DIFF_DV), BF16),
        scratch_shapes=[
            pltpu.VMEM((2, tq, 128), F32),
            pltpu.VMEM((2, tq, 128), F32),
            pltpu.VMEM((2, tq, DIFF_DV), F32),
        ],
        compiler_params=_params("arbitrary", "arbitrary", "arbitrary", "arbitrary"),
        name="diff_attention",
    )(*lam_params, subln, bias, qk, qk, vg, vg)


def _angles(Lp, rot_dim, theta):
    pos = jnp.arange(Lp, dtype=F32) - PAD_ROWS
    half = rot_dim // 2
    inv_freq = jnp.power(jnp.float32(theta), -jnp.arange(half, dtype=F32) * 2.0 / rot_dim)
    return pos[:, None] * inv_freq[None, :]


def _ret_tables(batch, Lp, dk):
    ang = _angles(Lp, dk, RET_THETA)
    tile = lambda t: jnp.tile(t, (batch, 1))
    return tile(jnp.cos(ang)), tile(jnp.sin(ang))


def _diff_tables(batch, Lp):
    ang = _angles(Lp, DIFF_ROT, ROPE_THETA)
    half = DIFF_ROT // 2
    cos, sin = jnp.cos(ang), jnp.sin(ang)
    zeros = lambda n: jnp.zeros((Lp, n), F32)
    c = jnp.concatenate([cos, cos, jnp.ones((Lp, DIFF_DH - DIFF_ROT), F32)], axis=1)
    s_up = jnp.concatenate([zeros(half), sin, zeros(DIFF_DH - DIFF_ROT)], axis=1)
    s_dn = jnp.concatenate([-sin, zeros(DIFF_DH - half)], axis=1)
    tile = lambda t: jnp.tile(t, (batch, 1))
    return tile(c), tile(s_up), tile(s_dn)


def _trunk(x, meta_block, pre_norm, post_norm, ret_w, diff_w, ret_decay, diff_small):
    B, S, D = x.shape
    assert S % ROW_ALIGN == 0
    n_chunks = S // ROW_ALIGN + 1
    Lp = n_chunks * ROW_ALIGN
    R = B * Lp
    bm = _divisor_block(R, MM_BLOCK_M, ROW_ALIGN)
    bn = MM_BLOCK_N

    x_cur, h = _embed_norm(x, meta_block, pre_norm[0][None])
    x_cur = x_cur.reshape(R, D)
    h = h.reshape(R, D)

    depth = pre_norm.shape[0]
    for i in range(depth):
        j = i // N_MIXERS
        if i % N_MIXERS == 0:
            w_qk, w_vg, w_out = ret_w[j]
            dk = D // RET_HEADS
            cos, sin = _ret_tables(B, Lp, dk)
            rot = functools.partial(_mm_rot_full_kernel, head_dim=dk,
                                    n_scaled_blocks=(RET_HEADS * dk) // bn, scale=dk ** -0.5)
            qk = _matmul(h, w_qk, bm=bm, bn=bn, out_dtype=BF16, body=rot,
                         row_tables=(cos, sin), name="ret_proj_qk")
            vg = _matmul(h, w_vg, bm=bm, bn=bn, out_dtype=BF16, name="ret_proj_vg")
            raw_f, raw_b = ret_decay[j]
            T = _divisor_block(n_chunks, RET_CHUNKS_PER_STEP, 1)
            o = _retention(qk, vg, raw_f, raw_b, batch=B, n_chunks=n_chunks, chunks_per_step=T)
        else:
            w_qk, w_vg, w_out = diff_w[j]
            lam_params, subln = diff_small[j]
            lambda_init = 0.8 - 0.6 * math.exp(-0.3 * i)
            tabs = _diff_tables(B, Lp)
            rot = functools.partial(_mm_rot_part_kernel, group=DIFF_DH, rot_half=DIFF_ROT // 2,
                                    n_scaled_blocks=(DIFF_HEADS * 2 * DIFF_DH) // bn,
                                    scale=DIFF_DH ** -0.5)
            qk = _matmul(h, w_qk, bm=bm, bn=bn, out_dtype=BF16, body=rot,
                         row_tables=tabs, name="diff_proj_qk")
            vg = _matmul(h, w_vg, bm=bm, bn=bn, out_dtype=BF16, name="diff_proj_vg")
            bias = jnp.where(jnp.arange(Lp) < PAD_ROWS, MASK_BIAS, 0.0).astype(F32)[None]
            tq = _divisor_block(Lp, ATTN_BLOCK_Q, 16)
            tk = _divisor_block(Lp, ATTN_BLOCK_K, 128)
            o = _diff_attention(qk, vg, lam_params, subln, bias, batch=B,
                                lambda_init=lambda_init, tq=tq, tk=tk)
        m = _matmul_ksplit(o, w_out, bm=bm, bn=bn, bk=o.shape[1] // 2, name="out_proj")
        if i + 1 < depth:
            x_cur, h = _resnorm(m, x_cur, post_norm[i][None], pre_norm[i + 1][None],
                                bm=_divisor_block(R, RESNORM_BLOCK_M, ROW_ALIGN))
        else:
            return _resnorm_final(m, x_cur, post_norm[i], batch=B)


def kernel(x_prompt, x_sample, meta_tokens, pre_norm, post_norm, ret_w_in, ret_w_out, ret_decay_fwd,
           ret_decay_bwd, diff_w_in, diff_w_out, diff_lambda_q1, diff_lambda_k1, diff_lambda_q2,
           diff_lambda_k2, diff_subln):
    D = x_prompt.shape[-1]
    meta_block = jnp.pad(meta_tokens.astype(F32), ((PAD_ROWS, 0), (0, 0)))

    ret_qk_cols = 2 * D
    ret_w = [(ret_w_in[j][:, :ret_qk_cols].astype(BF16), ret_w_in[j][:, ret_qk_cols:].astype(BF16),
              ret_w_out[j].astype(BF16)) for j in range(ret_w_in.shape[0])]
    diff_qk_cols = 2 * DIFF_HEADS * 2 * DIFF_DH
    diff_w = [(diff_w_in[j][:, :diff_qk_cols].astype(BF16), diff_w_in[j][:, diff_qk_cols:].astype(BF16),
               diff_w_out[j].astype(BF16)) for j in range(diff_w_in.shape[0])]
    bcast = lambda v: jnp.broadcast_to(v.astype(F32)[:, None, None], (v.shape[0], 8, 128))
    ret_decay = [(bcast(ret_decay_fwd[j]), bcast(ret_decay_bwd[j])) for j in range(ret_decay_fwd.shape[0])]
    diff_small = [((diff_lambda_q1[j][None], diff_lambda_k1[j][None], diff_lambda_q2[j][None],
                    diff_lambda_k2[j][None]), diff_subln[j][None]) for j in range(diff_subln.shape[0])]

    run = lambda x: _trunk(x, meta_block, pre_norm, post_norm, ret_w, diff_w, ret_decay, diff_small)
    return (run(x_prompt), run(x_sample))
```

```python
import functools
import math

import jax
import jax.numpy as jnp
from jax import lax
from jax.experimental import pallas as pl
from jax.experimental.pallas import tpu as pltpu

N_META = 16
N_MIXERS = 2
RET_HEADS = 16
RET_CHUNK = 128
RET_THETA = 10000.0
DIFF_HEADS = 32
DIFF_DH = 128
DIFF_DV = 2 * DIFF_DH
DIFF_ROT = DIFF_DH // 4
ROPE_THETA = 500000.0
NORM_EPS = 1e-6
SUBLN_EPS = 1e-5

ROW_ALIGN = 128
PAD_ROWS = ROW_ALIGN - N_META
P_STRIP_ROWS = 16
MASK_BIAS = -1e30
V7X_VMEM_LIMIT_BYTES = 56 * 1024 * 1024

MM_BLOCK_M = 1024
MM_BLOCK_N = 1024
RESNORM_BLOCK_M = 256
ATTN_BLOCK_Q = 2176
ATTN_BLOCK_K = 2176
ATTN_CHUNK_ROWS = 448
RET_CHUNKS_PER_STEP = 17

F32 = jnp.float32
BF16 = jnp.bfloat16


def _divisor_block(total, target, mult):
    best = None
    for cand in range(mult, min(total, target) + 1, mult):
        if total % cand == 0:
            best = cand
    assert best is not None, (total, target, mult)
    return best


def _params(*sem, flags=None):
    return pltpu.CompilerParams(dimension_semantics=sem, vmem_limit_bytes=V7X_VMEM_LIMIT_BYTES,
                                flags=flags)


def _rms(xf, gain, eps):
    y = xf * lax.rsqrt(jnp.mean(xf * xf, axis=-1, keepdims=True) + eps)
    if gain is not None:
        y = y * gain
    return y


def _silu(g):
    return g * (1.0 / (1.0 + jnp.exp(-g)))


def _embed_norm_kernel(x_ref, meta_ref, gain_ref, x0_ref, h_ref):
    t = pl.program_id(1)
    xb = jnp.where(t == 0, meta_ref[...], x_ref[0])
    x0_ref[0] = xb
    h_ref[0] = _rms(xb, gain_ref[...], NORM_EPS).astype(BF16)


def _embed_norm(x, meta_block, gain):
    B, S, D = x.shape
    n_blk = S // ROW_ALIGN + 1
    Lp = n_blk * ROW_ALIGN
    blk = (1, ROW_ALIGN, D)
    return pl.pallas_call(
        _embed_norm_kernel,
        grid=(B, n_blk),
        in_specs=[
            pl.BlockSpec(blk, lambda b, t: (b, jnp.maximum(t - 1, 0), 0)),
            pl.BlockSpec((ROW_ALIGN, D), lambda b, t: (0, 0)),
            pl.BlockSpec((1, D), lambda b, t: (0, 0)),
        ],
        out_specs=[pl.BlockSpec(blk, lambda b, t: (b, t, 0)),
                   pl.BlockSpec(blk, lambda b, t: (b, t, 0))],
        out_shape=[jax.ShapeDtypeStruct((B, Lp, D), F32),
                   jax.ShapeDtypeStruct((B, Lp, D), BF16)],
        compiler_params=_params("arbitrary", "arbitrary"),
        name="embed_norm",
    )(x, meta_block, gain)


def _mm_plain_kernel(x_ref, w_ref, o_ref):
    o_ref[...] = jnp.dot(x_ref[...], w_ref[...], preferred_element_type=F32).astype(o_ref.dtype)


def _mm_rot_full_kernel(x_ref, w_ref, cos_ref, sin_ref, o_ref, *, head_dim, n_scaled_blocks, scale):
    acc = jnp.dot(x_ref[...], w_ref[...], preferred_element_type=F32)
    sc = jnp.where(pl.program_id(1) < n_scaled_blocks, scale, 1.0).astype(F32)
    c = cos_ref[...] * sc
    s = sin_ref[...] * sc
    half = head_dim // 2
    for h0 in range(0, acc.shape[1], head_dim):
        x1 = acc[:, h0:h0 + half]
        x2 = acc[:, h0 + half:h0 + head_dim]
        o_ref[:, h0:h0 + half] = (x1 * c - x2 * s).astype(o_ref.dtype)
        o_ref[:, h0 + half:h0 + head_dim] = (x2 * c + x1 * s).astype(o_ref.dtype)


def _mm_rot_part_kernel(x_ref, w_ref, c_ref, s_up_ref, s_dn_ref, o_ref, *, group, rot_half,
                        n_scaled_blocks, scale):
    acc = jnp.dot(x_ref[...], w_ref[...], preferred_element_type=F32)
    sc = jnp.where(pl.program_id(1) < n_scaled_blocks, scale, 1.0).astype(F32)
    c = c_ref[...] * sc
    s_up = s_up_ref[...] * sc
    s_dn = s_dn_ref[...] * sc
    for g0 in range(0, acc.shape[1], group):
        xg = acc[:, g0:g0 + group]
        from_lo = pltpu.roll(xg, rot_half, 1)
        from_hi = pltpu.roll(xg, group - rot_half, 1)
        o_ref[:, g0:g0 + group] = (xg * c + from_lo * s_up + from_hi * s_dn).astype(o_ref.dtype)


def _matmul(x, w, *, bm, bn, out_dtype, body=_mm_plain_kernel, row_tables=(), name):
    M, K = x.shape
    N = w.shape[1]
    tab_specs = [pl.BlockSpec((bm, t.shape[1]), lambda i, j: (i, 0)) for t in row_tables]
    return pl.pallas_call(
        body,
        grid=(M // bm, N // bn),
        in_specs=[pl.BlockSpec((bm, K), lambda i, j: (i, 0)),
                  pl.BlockSpec((K, bn), lambda i, j: (0, j))] + tab_specs,
        out_specs=pl.BlockSpec((bm, bn), lambda i, j: (i, j)),
        out_shape=jax.ShapeDtypeStruct((M, N), out_dtype),
        compiler_params=_params("arbitrary", "arbitrary"),
        name=name,
    )(x, w, *row_tables)


def _mm_acc_kernel(x_ref, w_ref, o_ref):
    part = jnp.dot(x_ref[...], w_ref[...], preferred_element_type=F32)
    k = pl.program_id(2)

    @pl.when(k == 0)
    def _():
        o_ref[...] = part

    @pl.when(k != 0)
    def _():
        o_ref[...] += part


def _matmul_ksplit(x, w, *, bm, bn, bk, name):
    M, K = x.shape
    N = w.shape[1]
    return pl.pallas_call(
        _mm_acc_kernel,
        grid=(M // bm, N // bn, K // bk),
        in_specs=[pl.BlockSpec((bm, bk), lambda i, j, k: (i, k)),
                  pl.BlockSpec((bk, bn), lambda i, j, k: (k, j))],
        out_specs=pl.BlockSpec((bm, bn), lambda i, j, k: (i, j)),
        out_shape=jax.ShapeDtypeStruct((M, N), F32),
        compiler_params=_params("arbitrary", "arbitrary", "arbitrary"),
        name=name,
    )(x, w)


def _retention_kernel(rawf_ref, rawb_ref, q_ref, k_ref, v_ref, g_ref, o_ref,
                      state_ref, stash_ref, dmat_ref, qdec_ref, kdec_ref, cdec_ref,
                      *, chunks_per_step, n_steps):
    C = RET_CHUNK
    T = chunks_per_step
    phase = pl.program_id(2)
    step = pl.program_id(3)
    dk = q_ref.shape[1]
    dv = v_ref.shape[1]

    @pl.when((phase == 0) & (step == 0))
    def _():
        lgf = jnp.broadcast_to(-jnp.exp(rawf_ref[0])[0:1, :], (C, C))
        lgb = jnp.broadcast_to(-jnp.exp(rawb_ref[0])[0:1, :], (C, C))
        row = lax.broadcasted_iota(jnp.int32, (C, C), 0).astype(F32)
        col = lax.broadcasted_iota(jnp.int32, (C, C), 1).astype(F32)
        rel = row - col
        d_f = jnp.where(rel >= 0, jnp.exp(jnp.where(rel >= 0, rel, 0.0) * lgf), 0.0)
        d_b = jnp.where(rel < 0, jnp.exp(jnp.where(rel < 0, -rel, 0.0) * lgb), 0.0)
        dmat_ref[...] = d_f + d_b
        reps_k = dk // C
        qdec_ref[0] = jnp.concatenate([jnp.exp((row + 1.0) * lgf)] * reps_k, axis=1)
        kdec_ref[0] = jnp.concatenate([jnp.exp((C - 1.0 - row) * lgf)] * reps_k, axis=1)
        qdec_ref[1] = jnp.concatenate([jnp.exp((C - row) * lgb)] * reps_k, axis=1)
        kdec_ref[1] = jnp.concatenate([jnp.exp(row * lgb)] * reps_k, axis=1)
        reps_v = dv // C
        cdec_ref[0] = jnp.concatenate([jnp.exp(C * lgf[0:8, :])] * reps_v, axis=1)
        cdec_ref[1] = jnp.concatenate([jnp.exp(C * lgb[0:8, :])] * reps_v, axis=1)

    @pl.when(step == 0)
    def _():
        state_ref[...] = jnp.zeros_like(state_ref)

    def cross_and_update(direction, rows):
        qc = q_ref[rows, :]
        kc = k_ref[rows, :]
        vc = v_ref[rows, :]
        st = state_ref[...]
        qd = (qc.astype(F32) * qdec_ref[direction]).astype(BF16)
        cross = jnp.dot(qd, st.astype(BF16), preferred_element_type=F32)
        kd = (kc.astype(F32) * kdec_ref[direction]).astype(BF16)
        upd = lax.dot_general(kd, vc, (((0,), (0,)), ((), ())), preferred_element_type=F32)
        state_ref[...] = st * cdec_ref[direction][0:1, :] + upd
        return qc, kc, vc, cross

    @pl.when(phase == 0)
    def _():
        for c in range(T):
            rows = pl.ds(c * C, C)
            qc, kc, vc, cross = cross_and_update(0, rows)
            a = lax.dot_general(qc, kc, (((1,), (1,)), ((), ())), preferred_element_type=F32)
            pm = (a * dmat_ref[...]).astype(BF16)
            intra = jnp.dot(pm, vc, preferred_element_type=F32)
            stash_ref[step * T + c] = intra + cross

    @pl.when(phase == 1)
    def _():
        blk = n_steps - 1 - step
        for c in range(T - 1, -1, -1):
            rows = pl.ds(c * C, C)
            _, _, _, cross = cross_and_update(1, rows)
            o = stash_ref[blk * T + c] + cross
            y = _rms(o, None, NORM_EPS)
            y = y * _silu(g_ref[rows, :].astype(F32))
            o_ref[rows, :] = y.astype(o_ref.dtype)


def _retention(qk, vg, raw_f, raw_b, *, batch, n_chunks, chunks_per_step):
    H = RET_HEADS
    R = qk.shape[0]
    dk = qk.shape[1] // (2 * H)
    dv = vg.shape[1] // (2 * H)
    T = chunks_per_step
    NS = n_chunks // T
    rows = T * RET_CHUNK

    def sweep_block(b, p, s):
        return b * NS + jnp.where(p == 0, s, NS - 1 - s)

    def store_block(b, p, s):
        return b * NS + jnp.where(p == 0, NS - 1, NS - 1 - s)

    kern = functools.partial(_retention_kernel, chunks_per_step=T, n_steps=NS)
    return pl.pallas_call(
        kern,
        grid=(batch, H, 2, NS),
        in_specs=[
            pl.BlockSpec((1, 8, 128), lambda b, h, p, s: (h, 0, 0)),
            pl.BlockSpec((1, 8, 128), lambda b, h, p, s: (h, 0, 0)),
            pl.BlockSpec((rows, dk), lambda b, h, p, s: (sweep_block(b, p, s), h)),
            pl.BlockSpec((rows, dk), lambda b, h, p, s: (sweep_block(b, p, s), H + h)),
            pl.BlockSpec((rows, dv), lambda b, h, p, s: (sweep_block(b, p, s), h)),
            pl.BlockSpec((rows, dv), lambda b, h, p, s: (store_block(b, p, s), H + h)),
        ],
        out_specs=pl.BlockSpec((rows, dv), lambda b, h, p, s: (store_block(b, p, s), h)),
        out_shape=jax.ShapeDtypeStruct((R, H * dv), BF16),
        scratch_shapes=[
            pltpu.VMEM((dk, dv), F32),
            pltpu.VMEM((n_chunks, RET_CHUNK, dv), F32),
            pltpu.VMEM((RET_CHUNK, RET_CHUNK), F32),
            pltpu.VMEM((2, RET_CHUNK, dk), F32),
            pltpu.VMEM((2, RET_CHUNK, dk), F32),
            pltpu.VMEM((2, 8, dv), F32),
        ],
        compiler_params=_params("arbitrary", "arbitrary", "arbitrary", "arbitrary"),
        name="retention",
    )(raw_f, raw_b, qk, qk, vg, vg)


def _resnorm_kernel(m_ref, x_ref, post_ref, pre_ref, xo_ref, h_ref):
    xn = x_ref[...] + _rms(m_ref[...], post_ref[...], NORM_EPS)
    xo_ref[...] = xn
    h_ref[...] = _rms(xn, pre_ref[...], NORM_EPS).astype(BF16)


def _resnorm(m, x, post, pre, *, bm):
    R, D = m.shape
    row = pl.BlockSpec((bm, D), lambda i: (i, 0))
    vec = pl.BlockSpec((1, D), lambda i: (0, 0))
    return pl.pallas_call(
        _resnorm_kernel,
        grid=(R // bm,),
        in_specs=[row, row, vec, vec],
        out_specs=[row, row],
        out_shape=[jax.ShapeDtypeStruct((R, D), F32), jax.ShapeDtypeStruct((R, D), BF16)],
        compiler_params=_params("arbitrary"),
        name="resnorm",
    )(m, x, post, pre)


def _resnorm_final_kernel(m_ref, x_ref, post_ref, y_ref):
    y_ref[...] = x_ref[...] + _rms(m_ref[...], post_ref[...], NORM_EPS)


def _resnorm_final(m, x, post, *, batch):
    R, D = m.shape
    Lp = R // batch
    n_blk = Lp // ROW_ALIGN
    m3 = m.reshape(batch, Lp, D)
    x3 = x.reshape(batch, Lp, D)
    blk = (1, ROW_ALIGN, D)
    return pl.pallas_call(
        _resnorm_final_kernel,
        grid=(batch, n_blk - 1),
        in_specs=[pl.BlockSpec(blk, lambda b, t: (b, t + 1, 0)),
                  pl.BlockSpec(blk, lambda b, t: (b, t + 1, 0)),
                  pl.BlockSpec((1, 1, D), lambda b, t: (0, 0, 0))],
        out_specs=pl.BlockSpec(blk, lambda b, t: (b, t, 0)),
        out_shape=jax.ShapeDtypeStruct((batch, Lp - ROW_ALIGN, D), F32),
        compiler_params=_params("arbitrary", "arbitrary"),
        name="resnorm_final",
    )(m3, x3, post.reshape(1, 1, D))


def _diff_attn_kernel(lq1_ref, lk1_ref, lq2_ref, lk2_ref, subln_ref, bias_ref,
                      q_ref, k_ref, v_ref, g_ref, o_ref,
                      s0_ref, s1_ref, m_ref, l_ref, acc_ref,
                      *, lambda_init, key_rows, chunk_rows):
    dh = DIFF_DH
    lanes = ROW_ALIGN
    rc = chunk_rows
    tk = key_rows
    n_chunks = q_ref.shape[0] // rc
    n_items = (k_ref.shape[0] // tk) * n_chunks
    n_lane_tiles = tk // lanes
    s_bufs = (s0_ref, s1_ref)

    m_ref[...] = jnp.full_like(m_ref, -jnp.inf)
    l_ref[...] = jnp.zeros_like(l_ref)
    acc_ref[...] = jnp.zeros_like(acc_ref)

    def item(t):
        if isinstance(t, int):
            j, c = divmod(t, n_chunks)
            return j, j * tk, c * rc
        j = lax.div(t, n_chunks)
        c = lax.rem(t, n_chunks)
        return j, pl.multiple_of(j * tk, lanes), pl.multiple_of(c * rc, P_STRIP_ROWS)

    def score(t, buf):
        j, k0, q0 = item(t)
        bias = bias_ref[j]
        for mi in range(2):
            qm = q_ref[pl.ds(q0, rc), pl.ds(mi * dh, dh)]
            km = k_ref[pl.ds(k0, tk), pl.ds(mi * dh, dh)]
            s = lax.dot_general(qm, km, (((1,), (1,)), ((), ())), preferred_element_type=F32)
            s_bufs[buf][mi, :, :lanes] = s[:, :lanes] + bias
            if n_lane_tiles > 1:
                s_bufs[buf][mi, :, lanes:] = s[:, lanes:]

    def reduce(t, buf):
        _, k0, start = item(t)
        rows = pl.ds(start, rc)
        v = v_ref[pl.ds(k0, tk), :]
        for mi in range(2):
            s = s_bufs[buf][mi]
            m_prev = m_ref[mi, rows, :]
            m_new = jnp.maximum(m_prev, jnp.max(s, axis=1, keepdims=True))
            alpha = jnp.exp2(m_prev - m_new)
            m_ref[mi, rows, :] = m_new
            p = jnp.exp2(s - m_new[:, 0:1])
            row_part = p[:, :lanes]
            for ct in range(1, n_lane_tiles):
                row_part = row_part + p[:, ct * lanes:(ct + 1) * lanes]
            l_ref[mi, rows, :] = alpha * l_ref[mi, rows, :] + row_part
            pv = jnp.dot(p.astype(BF16), v, preferred_element_type=F32)
            acc_ref[mi, rows, :] = acc_ref[mi, rows, :] * jnp.concatenate([alpha, alpha], axis=1) + pv

    score(0, 0)

    def steady(t, carry):
        odd = lax.rem(t, 2) == 1

        @pl.when(jnp.logical_not(odd))
        def _():
            score(t + 1, 1)
            reduce(t, 0)

        @pl.when(odd)
        def _():
            score(t + 1, 0)
            reduce(t, 1)

        return carry

    lax.fori_loop(0, n_items - 1, steady, 0)
    reduce(n_items - 1, (n_items - 1) % 2)

    lam = (jnp.exp(jnp.sum(lq1_ref[...] * lk1_ref[...], axis=1, keepdims=True))
           - jnp.exp(jnp.sum(lq2_ref[...] * lk2_ref[...], axis=1, keepdims=True))
           + lambda_init)
    o1 = acc_ref[0] / jnp.sum(l_ref[0], axis=1, keepdims=True)
    o2 = acc_ref[1] / jnp.sum(l_ref[1], axis=1, keepdims=True)
    o = o1 - lam * o2
    y = _rms(o, subln_ref[...], SUBLN_EPS) * (1.0 - lambda_init)
    y = y * _silu(g_ref[...].astype(F32))
    o_ref[...] = y.astype(o_ref.dtype)


def _attn_tiling(Lp):
    rc = _divisor_block(Lp, ATTN_CHUNK_ROWS, P_STRIP_ROWS)
    chunks = Lp // rc
    n = max(c for c in range(2, chunks + 1) if chunks % c == 0 and c * rc <= ATTN_BLOCK_Q)
    return n * rc, _divisor_block(Lp, ATTN_BLOCK_K, ROW_ALIGN), rc


def _diff_attention(qk, vg, lam_params, subln, *, batch, lambda_init):
    H = DIFF_HEADS
    R = qk.shape[0]
    Lp = R // batch
    tq, tk, rc = _attn_tiling(Lp)
    nq = Lp // tq
    nk = Lp // tk
    w = 2 * DIFF_DH
    pad_key = (jnp.arange(nk)[:, None, None] == 0) & (jnp.arange(ROW_ALIGN)[None, None, :] < PAD_ROWS)
    bias = jnp.where(pad_key, MASK_BIAS, 0.0).astype(F32)
    vec = pl.BlockSpec((1, DIFF_DH), lambda b, h, i: (0, 0))
    kern = functools.partial(_diff_attn_kernel, lambda_init=lambda_init, key_rows=tk, chunk_rows=rc)
    return pl.pallas_call(
        kern,
        grid=(batch, H, nq),
        in_specs=[
            vec, vec, vec, vec,
            pl.BlockSpec((1, DIFF_DV), lambda b, h, i: (0, 0)),
            pl.BlockSpec((nk, 1, ROW_ALIGN), lambda b, h, i: (0, 0, 0)),
            pl.BlockSpec((tq, w), lambda b, h, i: (b * nq + i, h)),
            pl.BlockSpec((Lp, w), lambda b, h, i: (b, H + h)),
            pl.BlockSpec((Lp, DIFF_DV), lambda b, h, i: (b, h)),
            pl.BlockSpec((tq, DIFF_DV), lambda b, h, i: (b * nq + i, H + h)),
        ],
        out_specs=pl.BlockSpec((tq, DIFF_DV), lambda b, h, i: (b * nq + i, h)),
        out_shape=jax.ShapeDtypeStruct((R, H * DIFF_DV), BF16),
        scratch_shapes=[
            pltpu.VMEM((2, rc, tk), F32),
            pltpu.VMEM((2, rc, tk), F32),
            pltpu.VMEM((2, tq, 128), F32),
            pltpu.VMEM((2, tq, 128), F32),
            pltpu.VMEM((2, tq, DIFF_DV), F32),
        ],
        compiler_params=_params("arbitrary", "arbitrary", "arbitrary"),
        name="diff_attention",
    )(*lam_params, subln, bias, qk, qk, vg, vg)


def _angles(Lp, rot_dim, theta):
    pos = jnp.arange(Lp, dtype=F32) - PAD_ROWS
    half = rot_dim // 2
    inv_freq = jnp.power(jnp.float32(theta), -jnp.arange(half, dtype=F32) * 2.0 / rot_dim)
    return pos[:, None] * inv_freq[None, :]


def _ret_tables(batch, Lp, dk):
    ang = _angles(Lp, dk, RET_THETA)
    tile = lambda t: jnp.tile(t, (batch, 1))
    return tile(jnp.cos(ang)), tile(jnp.sin(ang))


def _diff_tables(batch, Lp):
    ang = _angles(Lp, DIFF_ROT, ROPE_THETA)
    half = DIFF_ROT // 2
    cos, sin = jnp.cos(ang), jnp.sin(ang)
    zeros = lambda n: jnp.zeros((Lp, n), F32)
    c = jnp.concatenate([cos, cos, jnp.ones((Lp, DIFF_DH - DIFF_ROT), F32)], axis=1)
    s_up = jnp.concatenate([zeros(half), sin, zeros(DIFF_DH - DIFF_ROT)], axis=1)
    s_dn = jnp.concatenate([-sin, zeros(DIFF_DH - half)], axis=1)
    tile = lambda t: jnp.tile(t, (batch, 1))
    return tile(c), tile(s_up), tile(s_dn)


def _trunk(x, meta_block, pre_norm, post_norm, ret_w, diff_w, ret_decay, diff_small):
    B, S, D = x.shape
    assert S % ROW_ALIGN == 0
    n_chunks = S // ROW_ALIGN + 1
    Lp = n_chunks * ROW_ALIGN
    R = B * Lp
    bm = _divisor_block(R, MM_BLOCK_M, ROW_ALIGN)
    bn = MM_BLOCK_N

    x_cur, h = _embed_norm(x, meta_block, pre_norm[0][None])
    x_cur = x_cur.reshape(R, D)
    h = h.reshape(R, D)

    depth = pre_norm.shape[0]
    for i in range(depth):
        j = i // N_MIXERS
        if i % N_MIXERS == 0:
            w_qk, w_vg, w_out = ret_w[j]
            dk = D // RET_HEADS
            cos, sin = _ret_tables(B, Lp, dk)
            rot = functools.partial(_mm_rot_full_kernel, head_dim=dk,
                                    n_scaled_blocks=(RET_HEADS * dk) // bn, scale=dk ** -0.5)
            qk = _matmul(h, w_qk, bm=bm, bn=bn, out_dtype=BF16, body=rot,
                         row_tables=(cos, sin), name="ret_proj_qk")
            vg = _matmul(h, w_vg, bm=bm, bn=bn, out_dtype=BF16, name="ret_proj_vg")
            raw_f, raw_b = ret_decay[j]
            T = _divisor_block(n_chunks, RET_CHUNKS_PER_STEP, 1)
            o = _retention(qk, vg, raw_f, raw_b, batch=B, n_chunks=n_chunks, chunks_per_step=T)
        else:
            w_qk, w_vg, w_out = diff_w[j]
            lam_params, subln = diff_small[j]
            lambda_init = 0.8 - 0.6 * math.exp(-0.3 * i)
            tabs = _diff_tables(B, Lp)
            rot = functools.partial(_mm_rot_part_kernel, group=DIFF_DH, rot_half=DIFF_ROT // 2,
                                    n_scaled_blocks=(DIFF_HEADS * 2 * DIFF_DH) // bn,
                                    scale=DIFF_DH ** -0.5 * math.log2(math.e))
            qk = _matmul(h, w_qk, bm=bm, bn=bn, out_dtype=BF16, body=rot,
                         row_tables=tabs, name="diff_proj_qk")
            vg = _matmul(h, w_vg, bm=bm, bn=bn, out_dtype=BF16, name="diff_proj_vg")
            o = _diff_attention(qk, vg, lam_params, subln, batch=B, lambda_init=lambda_init)
        m = _matmul_ksplit(o, w_out, bm=bm, bn=bn, bk=o.shape[1] // 2, name="out_proj")
        if i + 1 < depth:
            x_cur, h = _resnorm(m, x_cur, post_norm[i][None], pre_norm[i + 1][None],
                                bm=_divisor_block(R, RESNORM_BLOCK_M, ROW_ALIGN))
        else:
            return _resnorm_final(m, x_cur, post_norm[i], batch=B)


def kernel(x_prompt, x_sample, meta_tokens, pre_norm, post_norm, ret_w_in, ret_w_out, ret_decay_fwd,
           ret_decay_bwd, diff_w_in, diff_w_out, diff_lambda_q1, diff_lambda_k1, diff_lambda_q2,
           diff_lambda_k2, diff_subln):
    D = x_prompt.shape[-1]
    meta_block = jnp.pad(meta_tokens.astype(F32), ((PAD_ROWS, 0), (0, 0)))

    ret_qk_cols = 2 * D
    ret_w = [(ret_w_in[j][:, :ret_qk_cols].astype(BF16), ret_w_in[j][:, ret_qk_cols:].astype(BF16),
              ret_w_out[j].astype(BF16)) for j in range(ret_w_in.shape[0])]
    diff_qk_cols = 2 * DIFF_HEADS * 2 * DIFF_DH
    diff_w = [(diff_w_in[j][:, :diff_qk_cols].astype(BF16), diff_w_in[j][:, diff_qk_cols:].astype(BF16),
               diff_w_out[j].astype(BF16)) for j in range(diff_w_in.shape[0])]
    bcast = lambda v: jnp.broadcast_to(v.astype(F32)[:, None, None], (v.shape[0], 8, 128))
    ret_decay = [(bcast(ret_decay_fwd[j]), bcast(ret_decay_bwd[j])) for j in range(ret_decay_fwd.shape[0])]
    diff_small = [((diff_lambda_q1[j][None], diff_lambda_k1[j][None], diff_lambda_q2[j][None],
                    diff_lambda_k2[j][None]), diff_subln[j][None]) for j in range(diff_subln.shape[0])]

    run = lambda x: _trunk(x, meta_block, pre_norm, post_norm, ret_w, diff_w, ret_decay, diff_small)
    return (run(x_prompt), run(x_sample))
```

```python
import functools
import math

import jax
import jax.numpy as jnp
from jax import lax
from jax.experimental import pallas as pl
from jax.experimental.pallas import tpu as pltpu

N_META = 16
N_MIXERS = 2
RET_HEADS = 16
RET_CHUNK = 128
RET_THETA = 10000.0
DIFF_HEADS = 32
DIFF_DH = 128
DIFF_DV = 2 * DIFF_DH
DIFF_ROT = DIFF_DH // 4
ROPE_THETA = 500000.0
NORM_EPS = 1e-6
SUBLN_EPS = 1e-5

ROW_ALIGN = 128
PAD_ROWS = ROW_ALIGN - N_META
P_STRIP_ROWS = 16
MASK_BIAS = -1e30
V7X_VMEM_LIMIT_BYTES = 56 * 1024 * 1024

MM_BLOCK_M = 1024
MM_BLOCK_N = 1024
RESNORM_BLOCK_M = 256
ATTN_BLOCK_Q = 2176
ATTN_BLOCK_K = 2176
ATTN_CHUNK_ROWS = 448
RET_CHUNKS_PER_STEP = 17

F32 = jnp.float32
BF16 = jnp.bfloat16


def _divisor_block(total, target, mult):
    best = None
    for cand in range(mult, min(total, target) + 1, mult):
        if total % cand == 0:
            best = cand
    assert best is not None, (total, target, mult)
    return best


def _params(*sem, flags=None):
    return pltpu.CompilerParams(dimension_semantics=sem, vmem_limit_bytes=V7X_VMEM_LIMIT_BYTES,
                                flags=flags)


def _rms(xf, gain, eps):
    y = xf * lax.rsqrt(jnp.mean(xf * xf, axis=-1, keepdims=True) + eps)
    if gain is not None:
        y = y * gain
    return y


def _silu(g):
    return g * (1.0 / (1.0 + jnp.exp(-g)))


def _embed_norm_kernel(x_ref, meta_ref, gain_ref, x0_ref, h_ref):
    t = pl.program_id(1)
    xb = jnp.where(t == 0, meta_ref[...], x_ref[0])
    x0_ref[0] = xb
    h_ref[0] = _rms(xb, gain_ref[...], NORM_EPS).astype(BF16)


def _embed_norm(x, meta_block, gain):
    B, S, D = x.shape
    n_blk = S // ROW_ALIGN + 1
    Lp = n_blk * ROW_ALIGN
    blk = (1, ROW_ALIGN, D)
    return pl.pallas_call(
        _embed_norm_kernel,
        grid=(B, n_blk),
        in_specs=[
            pl.BlockSpec(blk, lambda b, t: (b, jnp.maximum(t - 1, 0), 0)),
            pl.BlockSpec((ROW_ALIGN, D), lambda b, t: (0, 0)),
            pl.BlockSpec((1, D), lambda b, t: (0, 0)),
        ],
        out_specs=[pl.BlockSpec(blk, lambda b, t: (b, t, 0)),
                   pl.BlockSpec(blk, lambda b, t: (b, t, 0))],
        out_shape=[jax.ShapeDtypeStruct((B, Lp, D), F32),
                   jax.ShapeDtypeStruct((B, Lp, D), BF16)],
        compiler_params=_params("arbitrary", "arbitrary"),
        name="embed_norm",
    )(x, meta_block, gain)


def _mm_plain_kernel(x_ref, w_ref, o_ref):
    o_ref[...] = jnp.dot(x_ref[...], w_ref[...], preferred_element_type=F32).astype(o_ref.dtype)


def _mm_rot_full_kernel(x_ref, w_ref, cos_ref, sin_ref, o_ref, *, head_dim, n_scaled_blocks, scale):
    acc = jnp.dot(x_ref[...], w_ref[...], preferred_element_type=F32)
    sc = jnp.where(pl.program_id(1) < n_scaled_blocks, scale, 1.0).astype(F32)
    c = cos_ref[...] * sc
    s = sin_ref[...] * sc
    half = head_dim // 2
    for h0 in range(0, acc.shape[1], head_dim):
        x1 = acc[:, h0:h0 + half]
        x2 = acc[:, h0 + half:h0 + head_dim]
        o_ref[:, h0:h0 + half] = (x1 * c - x2 * s).astype(o_ref.dtype)
        o_ref[:, h0 + half:h0 + head_dim] = (x2 * c + x1 * s).astype(o_ref.dtype)


def _mm_rot_part_kernel(x_ref, w_ref, c_ref, s_ref, o_ref, *, group, n_scaled_blocks, scale):
    acc = jnp.dot(x_ref[...], w_ref[...], preferred_element_type=F32)
    sc = jnp.where(pl.program_id(1) < n_scaled_blocks, scale, 1.0).astype(F32)
    c = c_ref[...] * sc
    s = s_ref[...] * sc
    for g0 in range(0, acc.shape[1], group):
        xg = acc[:, g0:g0 + group]
        partner = pltpu.roll(xg, group // 2, 1)
        o_ref[:, g0:g0 + group] = (xg * c + partner * s).astype(o_ref.dtype)


def _rotary_half_swap(w, group, rot_dim):
    half = rot_dim // 2
    k_in = w.shape[0]
    g = w.reshape(k_in, -1, group)
    g = jnp.concatenate([g[..., :half], g[..., rot_dim:group // 2 + half], g[..., half:rot_dim],
                         g[..., group // 2 + half:]], axis=-1)
    return g.reshape(k_in, -1)


def _matmul(x, w, *, bm, bn, out_dtype, body=_mm_plain_kernel, row_tables=(), name):
    M, K = x.shape
    N = w.shape[1]
    tab_specs = [pl.BlockSpec((bm, t.shape[1]), lambda i, j: (i, 0)) for t in row_tables]
    return pl.pallas_call(
        body,
        grid=(M // bm, N // bn),
        in_specs=[pl.BlockSpec((bm, K), lambda i, j: (i, 0)),
                  pl.BlockSpec((K, bn), lambda i, j: (0, j))] + tab_specs,
        out_specs=pl.BlockSpec((bm, bn), lambda i, j: (i, j)),
        out_shape=jax.ShapeDtypeStruct((M, N), out_dtype),
        compiler_params=_params("arbitrary", "arbitrary"),
        name=name,
    )(x, w, *row_tables)


def _mm_acc_kernel(x_ref, w_ref, o_ref):
    part = jnp.dot(x_ref[...], w_ref[...], preferred_element_type=F32)
    k = pl.program_id(2)

    @pl.when(k == 0)
    def _():
        o_ref[...] = part

    @pl.when(k != 0)
    def _():
        o_ref[...] += part


def _matmul_ksplit(x, w, *, bm, bn, bk, name):
    M, K = x.shape
    N = w.shape[1]
    return pl.pallas_call(
        _mm_acc_kernel,
        grid=(M // bm, N // bn, K // bk),
        in_specs=[pl.BlockSpec((bm, bk), lambda i, j, k: (i, k)),
                  pl.BlockSpec((bk, bn), lambda i, j, k: (k, j))],
        out_specs=pl.BlockSpec((bm, bn), lambda i, j, k: (i, j)),
        out_shape=jax.ShapeDtypeStruct((M, N), F32),
        compiler_params=_params("arbitrary", "arbitrary", "arbitrary"),
        name=name,
    )(x, w)


def _retention_kernel(rawf_ref, rawb_ref, q_ref, k_ref, v_ref, g_ref, o_ref,
                      state_ref, stash_ref, dmat_ref, qdec_ref, kdec_ref, cdec_ref,
                      *, chunks_per_step, n_steps):
    C = RET_CHUNK
    T = chunks_per_step
    phase = pl.program_id(2)
    step = pl.program_id(3)
    dk = q_ref.shape[1]
    dv = v_ref.shape[1]

    @pl.when((phase == 0) & (step == 0))
    def _():
        lgf = jnp.broadcast_to(-jnp.exp(rawf_ref[0])[0:1, :], (C, C))
        lgb = jnp.broadcast_to(-jnp.exp(rawb_ref[0])[0:1, :], (C, C))
        row = lax.broadcasted_iota(jnp.int32, (C, C), 0).astype(F32)
        col = lax.broadcasted_iota(jnp.int32, (C, C), 1).astype(F32)
        rel = row - col
        d_f = jnp.where(rel >= 0, jnp.exp(jnp.where(rel >= 0, rel, 0.0) * lgf), 0.0)
        d_b = jnp.where(rel < 0, jnp.exp(jnp.where(rel < 0, -rel, 0.0) * lgb), 0.0)
        dmat_ref[...] = d_f + d_b
        reps_k = dk // C
        qdec_ref[0] = jnp.concatenate([jnp.exp((row + 1.0) * lgf)] * reps_k, axis=1)
        kdec_ref[0] = jnp.concatenate([jnp.exp((C - 1.0 - row) * lgf)] * reps_k, axis=1)
        qdec_ref[1] = jnp.concatenate([jnp.exp((C - row) * lgb)] * reps_k, axis=1)
        kdec_ref[1] = jnp.concatenate([jnp.exp(row * lgb)] * reps_k, axis=1)
        reps_v = dv // C
        cdec_ref[0] = jnp.concatenate([jnp.exp(C * lgf[0:8, :])] * reps_v, axis=1)
        cdec_ref[1] = jnp.concatenate([jnp.exp(C * lgb[0:8, :])] * reps_v, axis=1)

    @pl.when(step == 0)
    def _():
        state_ref[...] = jnp.zeros_like(state_ref)

    def cross_and_update(direction, rows):
        qc = q_ref[rows, :]
        kc = k_ref[rows, :]
        vc = v_ref[rows, :]
        st = state_ref[...]
        qd = (qc.astype(F32) * qdec_ref[direction]).astype(BF16)
        cross = jnp.dot(qd, st.astype(BF16), preferred_element_type=F32)
        kd = (kc.astype(F32) * kdec_ref[direction]).astype(BF16)
        upd = lax.dot_general(kd, vc, (((0,), (0,)), ((), ())), preferred_element_type=F32)
        state_ref[...] = st * cdec_ref[direction][0:1, :] + upd
        return qc, kc, vc, cross

    @pl.when(phase == 0)
    def _():
        for c in range(T):
            rows = pl.ds(c * C, C)
            qc, kc, vc, cross = cross_and_update(0, rows)
            a = lax.dot_general(qc, kc, (((1,), (1,)), ((), ())), preferred_element_type=F32)
            pm = (a * dmat_ref[...]).astype(BF16)
            intra = jnp.dot(pm, vc, preferred_element_type=F32)
            stash_ref[step * T + c] = intra + cross

    @pl.when(phase == 1)
    def _():
        blk = n_steps - 1 - step
        for c in range(T - 1, -1, -1):
            rows = pl.ds(c * C, C)
            _, _, _, cross = cross_and_update(1, rows)
            o = stash_ref[blk * T + c] + cross
            y = _rms(o, None, NORM_EPS)
            y = y * _silu(g_ref[rows, :].astype(F32))
            o_ref[rows, :] = y.astype(o_ref.dtype)


def _retention(qk, vg, raw_f, raw_b, *, batch, n_chunks, chunks_per_step):
    H = RET_HEADS
    R = qk.shape[0]
    dk = qk.shape[1] // (2 * H)
    dv = vg.shape[1] // (2 * H)
    T = chunks_per_step
    NS = n_chunks // T
    rows = T * RET_CHUNK

    def sweep_block(b, p, s):
        return b * NS + jnp.where(p == 0, s, NS - 1 - s)

    def store_block(b, p, s):
        return b * NS + jnp.where(p == 0, NS - 1, NS - 1 - s)

    kern = functools.partial(_retention_kernel, chunks_per_step=T, n_steps=NS)
    return pl.pallas_call(
        kern,
        grid=(batch, H, 2, NS),
        in_specs=[
            pl.BlockSpec((1, 8, 128), lambda b, h, p, s: (h, 0, 0)),
            pl.BlockSpec((1, 8, 128), lambda b, h, p, s: (h, 0, 0)),
            pl.BlockSpec((rows, dk), lambda b, h, p, s: (sweep_block(b, p, s), h)),
            pl.BlockSpec((rows, dk), lambda b, h, p, s: (sweep_block(b, p, s), H + h)),
            pl.BlockSpec((rows, dv), lambda b, h, p, s: (sweep_block(b, p, s), h)),
            pl.BlockSpec((rows, dv), lambda b, h, p, s: (store_block(b, p, s), H + h)),
        ],
        out_specs=pl.BlockSpec((rows, dv), lambda b, h, p, s: (store_block(b, p, s), h)),
        out_shape=jax.ShapeDtypeStruct((R, H * dv), BF16),
        scratch_shapes=[
            pltpu.VMEM((dk, dv), F32),
            pltpu.VMEM((n_chunks, RET_CHUNK, dv), F32),
            pltpu.VMEM((RET_CHUNK, RET_CHUNK), F32),
            pltpu.VMEM((2, RET_CHUNK, dk), F32),
            pltpu.VMEM((2, RET_CHUNK, dk), F32),
            pltpu.VMEM((2, 8, dv), F32),
        ],
        compiler_params=_params("arbitrary", "arbitrary", "arbitrary", "arbitrary"),
        name="retention",
    )(raw_f, raw_b, qk, qk, vg, vg)


def _resnorm_kernel(m_ref, x_ref, post_ref, pre_ref, xo_ref, h_ref):
    xn = x_ref[...] + _rms(m_ref[...], post_ref[...], NORM_EPS)
    xo_ref[...] = xn
    h_ref[...] = _rms(xn, pre_ref[...], NORM_EPS).astype(BF16)


def _resnorm(m, x, post, pre, *, bm):
    R, D = m.shape
    row = pl.BlockSpec((bm, D), lambda i: (i, 0))
    vec = pl.BlockSpec((1, D), lambda i: (0, 0))
    return pl.pallas_call(
        _resnorm_kernel,
        grid=(R // bm,),
        in_specs=[row, row, vec, vec],
        out_specs=[row, row],
        out_shape=[jax.ShapeDtypeStruct((R, D), F32), jax.ShapeDtypeStruct((R, D), BF16)],
        compiler_params=_params("arbitrary"),
        name="resnorm",
    )(m, x, post, pre)


def _resnorm_final_kernel(m_ref, x_ref, post_ref, y_ref):
    y_ref[...] = x_ref[...] + _rms(m_ref[...], post_ref[...], NORM_EPS)


def _resnorm_final(m, x, post, *, batch):
    R, D = m.shape
    Lp = R // batch
    n_blk = Lp // ROW_ALIGN
    m3 = m.reshape(batch, Lp, D)
    x3 = x.reshape(batch, Lp, D)
    blk = (1, ROW_ALIGN, D)
    return pl.pallas_call(
        _resnorm_final_kernel,
        grid=(batch, n_blk - 1),
        in_specs=[pl.BlockSpec(blk, lambda b, t: (b, t + 1, 0)),
                  pl.BlockSpec(blk, lambda b, t: (b, t + 1, 0)),
                  pl.BlockSpec((1, 1, D), lambda b, t: (0, 0, 0))],
        out_specs=pl.BlockSpec(blk, lambda b, t: (b, t, 0)),
        out_shape=jax.ShapeDtypeStruct((batch, Lp - ROW_ALIGN, D), F32),
        compiler_params=_params("arbitrary", "arbitrary"),
        name="resnorm_final",
    )(m3, x3, post.reshape(1, 1, D))


def _diff_attn_kernel(lq1_ref, lk1_ref, lq2_ref, lk2_ref, subln_ref, bias_ref,
                      q_ref, k_ref, v_ref, g_ref, o_ref,
                      s0_ref, s1_ref, m_ref, l_ref, acc_ref,
                      *, lambda_init, key_rows, chunk_rows):
    dh = DIFF_DH
    lanes = ROW_ALIGN
    rc = chunk_rows
    tk = key_rows
    n_chunks = q_ref.shape[0] // rc
    n_items = (k_ref.shape[0] // tk) * n_chunks
    n_lane_tiles = tk // lanes
    s_bufs = (s0_ref, s1_ref)

    m_ref[...] = jnp.full_like(m_ref, -jnp.inf)
    l_ref[...] = jnp.zeros_like(l_ref)
    acc_ref[...] = jnp.zeros_like(acc_ref)

    def item(t):
        if isinstance(t, int):
            j, c = divmod(t, n_chunks)
            return j, j * tk, c * rc
        j = lax.div(t, n_chunks)
        c = lax.rem(t, n_chunks)
        return j, pl.multiple_of(j * tk, lanes), pl.multiple_of(c * rc, P_STRIP_ROWS)

    def score(t, buf):
        j, k0, q0 = item(t)
        bias = bias_ref[j]
        for mi in range(2):
            qm = q_ref[pl.ds(q0, rc), pl.ds(mi * dh, dh)]
            km = k_ref[pl.ds(k0, tk), pl.ds(mi * dh, dh)]
            s = lax.dot_general(qm, km, (((1,), (1,)), ((), ())), preferred_element_type=F32)
            s_bufs[buf][mi, :, :lanes] = s[:, :lanes] + bias
            if n_lane_tiles > 1:
                s_bufs[buf][mi, :, lanes:] = s[:, lanes:]

    def reduce(t, buf):
        _, k0, start = item(t)
        rows = pl.ds(start, rc)
        v = v_ref[pl.ds(k0, tk), :]
        probs, alphas = [], []
        for mi in range(2):
            m_prev = m_ref[mi, rows, :]
            m_new = jnp.maximum(m_prev, jnp.max(s_bufs[buf][mi], axis=1, keepdims=True))
            alpha = jnp.exp2(m_prev - m_new)
            m_ref[mi, rows, :] = m_new
            p = jnp.exp2(s_bufs[buf][mi] - m_new[:, 0:1])
            row_part = p[:, :lanes]
            for ct in range(1, n_lane_tiles):
                row_part = row_part + p[:, ct * lanes:(ct + 1) * lanes]
            l_ref[mi, rows, :] = alpha * l_ref[mi, rows, :] + row_part
            probs.append(p.astype(BF16))
            alphas.append(alpha)
        both = jnp.dot(jnp.concatenate(probs, axis=0), v, preferred_element_type=F32)
        for mi in range(2):
            alpha = alphas[mi]
            acc_ref[mi, rows, :] = (acc_ref[mi, rows, :] * jnp.concatenate([alpha, alpha], axis=1)
                                    + both[mi * rc:(mi + 1) * rc])

    score(0, 0)

    def steady(t, carry):
        odd = lax.rem(t, 2) == 1

        @pl.when(jnp.logical_not(odd))
        def _():
            score(t + 1, 1)
            reduce(t, 0)

        @pl.when(odd)
        def _():
            score(t + 1, 0)
            reduce(t, 1)

        return carry

    lax.fori_loop(0, n_items - 1, steady, 0)
    reduce(n_items - 1, (n_items - 1) % 2)

    lam = (jnp.exp(jnp.sum(lq1_ref[...] * lk1_ref[...], axis=1, keepdims=True))
           - jnp.exp(jnp.sum(lq2_ref[...] * lk2_ref[...], axis=1, keepdims=True))
           + lambda_init)
    o1 = acc_ref[0] / jnp.sum(l_ref[0], axis=1, keepdims=True)
    o2 = acc_ref[1] / jnp.sum(l_ref[1], axis=1, keepdims=True)
    o = o1 - lam * o2
    y = _rms(o, subln_ref[...], SUBLN_EPS) * (1.0 - lambda_init)
    y = y * _silu(g_ref[...].astype(F32))
    o_ref[...] = y.astype(o_ref.dtype)


def _attn_tiling(Lp):
    rc = _divisor_block(Lp, ATTN_CHUNK_ROWS, P_STRIP_ROWS)
    chunks = Lp // rc
    n = max(c for c in range(2, chunks + 1) if chunks % c == 0 and c * rc <= ATTN_BLOCK_Q)
    return n * rc, _divisor_block(Lp, ATTN_BLOCK_K, ROW_ALIGN), rc


def _diff_attention(qk, vg, lam_params, subln, *, batch, lambda_init):
    H = DIFF_HEADS
    R = qk.shape[0]
    Lp = R // batch
    tq, tk, rc = _attn_tiling(Lp)
    nq = Lp // tq
    nk = Lp // tk
    w = 2 * DIFF_DH
    pad_key = (jnp.arange(nk)[:, None, None] == 0) & (jnp.arange(ROW_ALIGN)[None, None, :] < PAD_ROWS)
    bias = jnp.where(pad_key, MASK_BIAS, 0.0).astype(F32)
    vec = pl.BlockSpec((1, DIFF_DH), lambda b, h, i: (0, 0))
    kern = functools.partial(_diff_attn_kernel, lambda_init=lambda_init, key_rows=tk, chunk_rows=rc)
    return pl.pallas_call(
        kern,
        grid=(batch, H, nq),
        in_specs=[
            vec, vec, vec, vec,
            pl.BlockSpec((1, DIFF_DV), lambda b, h, i: (0, 0)),
            pl.BlockSpec((nk, 1, ROW_ALIGN), lambda b, h, i: (0, 0, 0)),
            pl.BlockSpec((tq, w), lambda b, h, i: (b * nq + i, h)),
            pl.BlockSpec((Lp, w), lambda b, h, i: (b, H + h)),
            pl.BlockSpec((Lp, DIFF_DV), lambda b, h, i: (b, h)),
            pl.BlockSpec((tq, DIFF_DV), lambda b, h, i: (b * nq + i, H + h)),
        ],
        out_specs=pl.BlockSpec((tq, DIFF_DV), lambda b, h, i: (b * nq + i, h)),
        out_shape=jax.ShapeDtypeStruct((R, H * DIFF_DV), BF16),
        scratch_shapes=[
            pltpu.VMEM((2, rc, tk), F32),
            pltpu.VMEM((2, rc, tk), F32),
            pltpu.VMEM((2, tq, 128), F32),
            pltpu.VMEM((2, tq, 128), F32),
            pltpu.VMEM((2, tq, DIFF_DV), F32),
        ],
        compiler_params=_params("arbitrary", "arbitrary", "arbitrary"),
        name="diff_attention",
    )(*lam_params, subln, bias, qk, qk, vg, vg)


def _angles(Lp, rot_dim, theta):
    pos = jnp.arange(Lp, dtype=F32) - PAD_ROWS
    half = rot_dim // 2
    inv_freq = jnp.power(jnp.float32(theta), -jnp.arange(half, dtype=F32) * 2.0 / rot_dim)
    return pos[:, None] * inv_freq[None, :]


def _ret_tables(batch, Lp, dk):
    ang = _angles(Lp, dk, RET_THETA)
    tile = lambda t: jnp.tile(t, (batch, 1))
    return tile(jnp.cos(ang)), tile(jnp.sin(ang))


def _diff_tables(batch, Lp):
    ang = _angles(Lp, DIFF_ROT, ROPE_THETA)
    half = DIFF_ROT // 2
    cos, sin = jnp.cos(ang), jnp.sin(ang)
    gap = DIFF_DH // 2 - half
    ones = jnp.ones((Lp, gap), F32)
    zeros = jnp.zeros((Lp, gap), F32)
    c = jnp.concatenate([cos, ones, cos, ones], axis=1)
    s = jnp.concatenate([-sin, zeros, sin, zeros], axis=1)
    tile = lambda t: jnp.tile(t, (batch, 1))
    return tile(c), tile(s)


def _trunk(x, meta_block, pre_norm, post_norm, ret_w, diff_w, ret_decay, diff_small):
    B, S, D = x.shape
    assert S % ROW_ALIGN == 0
    n_chunks = S // ROW_ALIGN + 1
    Lp = n_chunks * ROW_ALIGN
    R = B * Lp
    bm = _divisor_block(R, MM_BLOCK_M, ROW_ALIGN)
    bn = MM_BLOCK_N

    x_cur, h = _embed_norm(x, meta_block, pre_norm[0][None])
    x_cur = x_cur.reshape(R, D)
    h = h.reshape(R, D)

    depth = pre_norm.shape[0]
    for i in range(depth):
        j = i // N_MIXERS
        if i % N_MIXERS == 0:
            w_qk, w_vg, w_out = ret_w[j]
            dk = D // RET_HEADS
            cos, sin = _ret_tables(B, Lp, dk)
            rot = functools.partial(_mm_rot_full_kernel, head_dim=dk,
                                    n_scaled_blocks=(RET_HEADS * dk) // bn, scale=dk ** -0.5)
            qk = _matmul(h, w_qk, bm=bm, bn=bn, out_dtype=BF16, body=rot,
                         row_tables=(cos, sin), name="ret_proj_qk")
            vg = _matmul(h, w_vg, bm=bm, bn=bn, out_dtype=BF16, name="ret_proj_vg")
            raw_f, raw_b = ret_decay[j]
            T = _divisor_block(n_chunks, RET_CHUNKS_PER_STEP, 1)
            o = _retention(qk, vg, raw_f, raw_b, batch=B, n_chunks=n_chunks, chunks_per_step=T)
        else:
            w_qk, w_vg, w_out = diff_w[j]
            lam_params, subln = diff_small[j]
            lambda_init = 0.8 - 0.6 * math.exp(-0.3 * i)
            tabs = _diff_tables(B, Lp)
            rot = functools.partial(_mm_rot_part_kernel, group=DIFF_DH,
                                    n_scaled_blocks=(DIFF_HEADS * 2 * DIFF_DH) // bn,
                                    scale=DIFF_DH ** -0.5 * math.log2(math.e))
            qk = _matmul(h, w_qk, bm=bm, bn=bn, out_dtype=BF16, body=rot,
                         row_tables=tabs, name="diff_proj_qk")
            vg = _matmul(h, w_vg, bm=bm, bn=bn, out_dtype=BF16, name="diff_proj_vg")
            o = _diff_attention(qk, vg, lam_params, subln, batch=B, lambda_init=lambda_init)
        m = _matmul_ksplit(o, w_out, bm=bm, bn=bn, bk=o.shape[1] // 2, name="out_proj")
        if i + 1 < depth:
            x_cur, h = _resnorm(m, x_cur, post_norm[i][None], pre_norm[i + 1][None],
                                bm=_divisor_block(R, RESNORM_BLOCK_M, ROW_ALIGN))
        else:
            return _resnorm_final(m, x_cur, post_norm[i], batch=B)


def kernel(x_prompt, x_sample, meta_tokens, pre_norm, post_norm, ret_w_in, ret_w_out, ret_decay_fwd,
           ret_decay_bwd, diff_w_in, diff_w_out, diff_lambda_q1, diff_lambda_k1, diff_lambda_q2,
           diff_lambda_k2, diff_subln):
    D = x_prompt.shape[-1]
    meta_block = jnp.pad(meta_tokens.astype(F32), ((PAD_ROWS, 0), (0, 0)))

    ret_qk_cols = 2 * D
    ret_w = [(ret_w_in[j][:, :ret_qk_cols].astype(BF16), ret_w_in[j][:, ret_qk_cols:].astype(BF16),
              ret_w_out[j].astype(BF16)) for j in range(ret_w_in.shape[0])]
    diff_qk_cols = 2 * DIFF_HEADS * 2 * DIFF_DH
    diff_w = [(_rotary_half_swap(diff_w_in[j][:, :diff_qk_cols], DIFF_DH, DIFF_ROT).astype(BF16),
               diff_w_in[j][:, diff_qk_cols:].astype(BF16),
               diff_w_out[j].astype(BF16)) for j in range(diff_w_in.shape[0])]
    bcast = lambda v: jnp.broadcast_to(v.astype(F32)[:, None, None], (v.shape[0], 8, 128))
    ret_decay = [(bcast(ret_decay_fwd[j]), bcast(ret_decay_bwd[j])) for j in range(ret_decay_fwd.shape[0])]
    diff_small = [((diff_lambda_q1[j][None], diff_lambda_k1[j][None], diff_lambda_q2[j][None],
                    diff_lambda_k2[j][None]), diff_subln[j][None]) for j in range(diff_subln.shape[0])]

    run = lambda x: _trunk(x, meta_block, pre_norm, post_norm, ret_w, diff_w, ret_decay, diff_small)
    return (run(x_prompt), run(x_sample))
```

```python
import functools
import math

import jax
import jax.numpy as jnp
from jax import lax
from jax.experimental import pallas as pl
from jax.experimental.pallas import tpu as pltpu

N_META = 16
N_MIXERS = 2
RET_HEADS = 16
RET_CHUNK = 128
RET_THETA = 10000.0
DIFF_HEADS = 32
DIFF_DH = 128
DIFF_DV = 2 * DIFF_DH
DIFF_ROT = DIFF_DH // 4
ROPE_THETA = 500000.0
NORM_EPS = 1e-6
SUBLN_EPS = 1e-5

ROW_ALIGN = 128
PAD_ROWS = ROW_ALIGN - N_META
P_STRIP_ROWS = 16
MASK_BIAS = -1e30
V7X_VMEM_LIMIT_BYTES = 56 * 1024 * 1024

MM_BLOCK_M = 1024
MM_BLOCK_N = 1024
RESNORM_BLOCK_M = 256
ATTN_BLOCK_Q = 2176
ATTN_BLOCK_K = 2176
ATTN_CHUNK_ROWS = 544
RET_CHUNKS_PER_STEP = 17

F32 = jnp.float32
BF16 = jnp.bfloat16


def _divisor_block(total, target, mult):
    best = None
    for cand in range(mult, min(total, target) + 1, mult):
        if total % cand == 0:
            best = cand
    assert best is not None, (total, target, mult)
    return best


def _params(*sem, flags=None):
    return pltpu.CompilerParams(dimension_semantics=sem, vmem_limit_bytes=V7X_VMEM_LIMIT_BYTES,
                                flags=flags)


def _rms(xf, gain, eps):
    y = xf * lax.rsqrt(jnp.mean(xf * xf, axis=-1, keepdims=True) + eps)
    if gain is not None:
        y = y * gain
    return y


def _silu(g):
    return g * (1.0 / (1.0 + jnp.exp(-g)))


def _embed_norm_kernel(x_ref, meta_ref, gain_ref, x0_ref, h_ref):
    t = pl.program_id(1)
    xb = jnp.where(t == 0, meta_ref[...], x_ref[0])
    x0_ref[0] = xb
    h_ref[0] = _rms(xb, gain_ref[...], NORM_EPS).astype(BF16)


def _embed_norm(x, meta_block, gain):
    B, S, D = x.shape
    n_blk = S // ROW_ALIGN + 1
    Lp = n_blk * ROW_ALIGN
    blk = (1, ROW_ALIGN, D)
    return pl.pallas_call(
        _embed_norm_kernel,
        grid=(B, n_blk),
        in_specs=[
            pl.BlockSpec(blk, lambda b, t: (b, jnp.maximum(t - 1, 0), 0)),
            pl.BlockSpec((ROW_ALIGN, D), lambda b, t: (0, 0)),
            pl.BlockSpec((1, D), lambda b, t: (0, 0)),
        ],
        out_specs=[pl.BlockSpec(blk, lambda b, t: (b, t, 0)),
                   pl.BlockSpec(blk, lambda b, t: (b, t, 0))],
        out_shape=[jax.ShapeDtypeStruct((B, Lp, D), F32),
                   jax.ShapeDtypeStruct((B, Lp, D), BF16)],
        compiler_params=_params("arbitrary", "arbitrary"),
        name="embed_norm",
    )(x, meta_block, gain)


def _mm_plain_kernel(x_ref, w_ref, o_ref):
    o_ref[...] = jnp.dot(x_ref[...], w_ref[...], preferred_element_type=F32).astype(o_ref.dtype)


def _mm_rot_full_kernel(x_ref, w_ref, cos_ref, sin_ref, o_ref, *, head_dim, n_scaled_blocks, scale):
    acc = jnp.dot(x_ref[...], w_ref[...], preferred_element_type=F32)
    sc = jnp.where(pl.program_id(1) < n_scaled_blocks, scale, 1.0).astype(F32)
    c = cos_ref[...] * sc
    s = sin_ref[...] * sc
    half = head_dim // 2
    for h0 in range(0, acc.shape[1], head_dim):
        x1 = acc[:, h0:h0 + half]
        x2 = acc[:, h0 + half:h0 + head_dim]
        o_ref[:, h0:h0 + half] = (x1 * c - x2 * s).astype(o_ref.dtype)
        o_ref[:, h0 + half:h0 + head_dim] = (x2 * c + x1 * s).astype(o_ref.dtype)


def _mm_rot_part_kernel(x_ref, w_ref, c_ref, s_ref, o_ref, *, group, n_scaled_blocks, scale):
    acc = jnp.dot(x_ref[...], w_ref[...], preferred_element_type=F32)
    sc = jnp.where(pl.program_id(1) < n_scaled_blocks, scale, 1.0).astype(F32)
    c = c_ref[...] * sc
    s = s_ref[...] * sc
    for g0 in range(0, acc.shape[1], group):
        xg = acc[:, g0:g0 + group]
        partner = pltpu.roll(xg, group // 2, 1)
        o_ref[:, g0:g0 + group] = (xg * c + partner * s).astype(o_ref.dtype)


def _rotary_half_swap(w, group, rot_dim):
    half = rot_dim // 2
    k_in = w.shape[0]
    g = w.reshape(k_in, -1, group)
    g = jnp.concatenate([g[..., :half], g[..., rot_dim:group // 2 + half], g[..., half:rot_dim],
                         g[..., group // 2 + half:]], axis=-1)
    return g.reshape(k_in, -1)


def _matmul(x, w, *, bm, bn, out_dtype, body=_mm_plain_kernel, row_tables=(), name):
    M, K = x.shape
    N = w.shape[1]
    tab_specs = [pl.BlockSpec((bm, t.shape[1]), lambda i, j: (i, 0)) for t in row_tables]
    return pl.pallas_call(
        body,
        grid=(M // bm, N // bn),
        in_specs=[pl.BlockSpec((bm, K), lambda i, j: (i, 0)),
                  pl.BlockSpec((K, bn), lambda i, j: (0, j))] + tab_specs,
        out_specs=pl.BlockSpec((bm, bn), lambda i, j: (i, j)),
        out_shape=jax.ShapeDtypeStruct((M, N), out_dtype),
        compiler_params=_params("arbitrary", "arbitrary"),
        name=name,
    )(x, w, *row_tables)


def _mm_acc_kernel(x_ref, w_ref, o_ref):
    part = jnp.dot(x_ref[...], w_ref[...], preferred_element_type=F32)
    k = pl.program_id(2)

    @pl.when(k == 0)
    def _():
        o_ref[...] = part

    @pl.when(k != 0)
    def _():
        o_ref[...] += part


def _matmul_ksplit(x, w, *, bm, bn, bk, name):
    M, K = x.shape
    N = w.shape[1]
    return pl.pallas_call(
        _mm_acc_kernel,
        grid=(M // bm, N // bn, K // bk),
        in_specs=[pl.BlockSpec((bm, bk), lambda i, j, k: (i, k)),
                  pl.BlockSpec((bk, bn), lambda i, j, k: (k, j))],
        out_specs=pl.BlockSpec((bm, bn), lambda i, j, k: (i, j)),
        out_shape=jax.ShapeDtypeStruct((M, N), F32),
        compiler_params=_params("arbitrary", "arbitrary", "arbitrary"),
        name=name,
    )(x, w)


def _retention_kernel(rawf_ref, rawb_ref, q_ref, k_ref, v_ref, g_ref, o_ref,
                      state_ref, stash_ref, dmat_ref, qdec_ref, kdec_ref, cdec_ref,
                      *, chunks_per_step, n_steps):
    C = RET_CHUNK
    T = chunks_per_step
    phase = pl.program_id(2)
    step = pl.program_id(3)
    dk = q_ref.shape[1]
    dv = v_ref.shape[1]

    @pl.when((phase == 0) & (step == 0))
    def _():
        lgf = jnp.broadcast_to(-jnp.exp(rawf_ref[0])[0:1, :], (C, C))
        lgb = jnp.broadcast_to(-jnp.exp(rawb_ref[0])[0:1, :], (C, C))
        row = lax.broadcasted_iota(jnp.int32, (C, C), 0).astype(F32)
        col = lax.broadcasted_iota(jnp.int32, (C, C), 1).astype(F32)
        rel = row - col
        d_f = jnp.where(rel >= 0, jnp.exp(jnp.where(rel >= 0, rel, 0.0) * lgf), 0.0)
        d_b = jnp.where(rel < 0, jnp.exp(jnp.where(rel < 0, -rel, 0.0) * lgb), 0.0)
        dmat_ref[...] = d_f + d_b
        reps_k = dk // C
        qdec_ref[0] = jnp.concatenate([jnp.exp((row + 1.0) * lgf)] * reps_k, axis=1)
        kdec_ref[0] = jnp.concatenate([jnp.exp((C - 1.0 - row) * lgf)] * reps_k, axis=1)
        qdec_ref[1] = jnp.concatenate([jnp.exp((C - row) * lgb)] * reps_k, axis=1)
        kdec_ref[1] = jnp.concatenate([jnp.exp(row * lgb)] * reps_k, axis=1)
        reps_v = dv // C
        cdec_ref[0] = jnp.concatenate([jnp.exp(C * lgf[0:8, :])] * reps_v, axis=1)
        cdec_ref[1] = jnp.concatenate([jnp.exp(C * lgb[0:8, :])] * reps_v, axis=1)

    @pl.when(step == 0)
    def _():
        state_ref[...] = jnp.zeros_like(state_ref)

    def cross_and_update(direction, rows):
        qc = q_ref[rows, :]
        kc = k_ref[rows, :]
        vc = v_ref[rows, :]
        st = state_ref[...]
        qd = (qc.astype(F32) * qdec_ref[direction]).astype(BF16)
        cross = jnp.dot(qd, st.astype(BF16), preferred_element_type=F32)
        kd = (kc.astype(F32) * kdec_ref[direction]).astype(BF16)
        upd = lax.dot_general(kd, vc, (((0,), (0,)), ((), ())), preferred_element_type=F32)
        state_ref[...] = st * cdec_ref[direction][0:1, :] + upd
        return qc, kc, vc, cross

    @pl.when(phase == 0)
    def _():
        for c in range(T):
            rows = pl.ds(c * C, C)
            qc, kc, vc, cross = cross_and_update(0, rows)
            a = lax.dot_general(qc, kc, (((1,), (1,)), ((), ())), preferred_element_type=F32)
            pm = (a * dmat_ref[...]).astype(BF16)
            intra = jnp.dot(pm, vc, preferred_element_type=F32)
            stash_ref[step * T + c] = intra + cross

    @pl.when(phase == 1)
    def _():
        blk = n_steps - 1 - step
        for c in range(T - 1, -1, -1):
            rows = pl.ds(c * C, C)
            _, _, _, cross = cross_and_update(1, rows)
            o = stash_ref[blk * T + c] + cross
            y = _rms(o, None, NORM_EPS)
            y = y * _silu(g_ref[rows, :].astype(F32))
            o_ref[rows, :] = y.astype(o_ref.dtype)


def _retention(qk, vg, raw_f, raw_b, *, batch, n_chunks, chunks_per_step):
    H = RET_HEADS
    R = qk.shape[0]
    dk = qk.shape[1] // (2 * H)
    dv = vg.shape[1] // (2 * H)
    T = chunks_per_step
    NS = n_chunks // T
    rows = T * RET_CHUNK

    def sweep_block(b, p, s):
        return b * NS + jnp.where(p == 0, s, NS - 1 - s)

    def store_block(b, p, s):
        return b * NS + jnp.where(p == 0, NS - 1, NS - 1 - s)

    kern = functools.partial(_retention_kernel, chunks_per_step=T, n_steps=NS)
    return pl.pallas_call(
        kern,
        grid=(batch, H, 2, NS),
        in_specs=[
            pl.BlockSpec((1, 8, 128), lambda b, h, p, s: (h, 0, 0)),
            pl.BlockSpec((1, 8, 128), lambda b, h, p, s: (h, 0, 0)),
            pl.BlockSpec((rows, dk), lambda b, h, p, s: (sweep_block(b, p, s), h)),
            pl.BlockSpec((rows, dk), lambda b, h, p, s: (sweep_block(b, p, s), H + h)),
            pl.BlockSpec((rows, dv), lambda b, h, p, s: (sweep_block(b, p, s), h)),
            pl.BlockSpec((rows, dv), lambda b, h, p, s: (store_block(b, p, s), H + h)),
        ],
        out_specs=pl.BlockSpec((rows, dv), lambda b, h, p, s: (store_block(b, p, s), h)),
        out_shape=jax.ShapeDtypeStruct((R, H * dv), BF16),
        scratch_shapes=[
            pltpu.VMEM((dk, dv), F32),
            pltpu.VMEM((n_chunks, RET_CHUNK, dv), F32),
            pltpu.VMEM((RET_CHUNK, RET_CHUNK), F32),
            pltpu.VMEM((2, RET_CHUNK, dk), F32),
            pltpu.VMEM((2, RET_CHUNK, dk), F32),
            pltpu.VMEM((2, 8, dv), F32),
        ],
        compiler_params=_params("arbitrary", "arbitrary", "arbitrary", "arbitrary"),
        name="retention",
    )(raw_f, raw_b, qk, qk, vg, vg)


def _resnorm_kernel(m_ref, x_ref, post_ref, pre_ref, xo_ref, h_ref):
    xn = x_ref[...] + _rms(m_ref[...], post_ref[...], NORM_EPS)
    xo_ref[...] = xn
    h_ref[...] = _rms(xn, pre_ref[...], NORM_EPS).astype(BF16)


def _resnorm(m, x, post, pre, *, bm):
    R, D = m.shape
    row = pl.BlockSpec((bm, D), lambda i: (i, 0))
    vec = pl.BlockSpec((1, D), lambda i: (0, 0))
    return pl.pallas_call(
        _resnorm_kernel,
        grid=(R // bm,),
        in_specs=[row, row, vec, vec],
        out_specs=[row, row],
        out_shape=[jax.ShapeDtypeStruct((R, D), F32), jax.ShapeDtypeStruct((R, D), BF16)],
        compiler_params=_params("arbitrary"),
        name="resnorm",
    )(m, x, post, pre)


def _resnorm_final_kernel(m_ref, x_ref, post_ref, y_ref):
    y_ref[...] = x_ref[...] + _rms(m_ref[...], post_ref[...], NORM_EPS)


def _resnorm_final(m, x, post, *, batch):
    R, D = m.shape
    Lp = R // batch
    n_blk = Lp // ROW_ALIGN
    m3 = m.reshape(batch, Lp, D)
    x3 = x.reshape(batch, Lp, D)
    blk = (1, ROW_ALIGN, D)
    return pl.pallas_call(
        _resnorm_final_kernel,
        grid=(batch, n_blk - 1),
        in_specs=[pl.BlockSpec(blk, lambda b, t: (b, t + 1, 0)),
                  pl.BlockSpec(blk, lambda b, t: (b, t + 1, 0)),
                  pl.BlockSpec((1, 1, D), lambda b, t: (0, 0, 0))],
        out_specs=pl.BlockSpec(blk, lambda b, t: (b, t, 0)),
        out_shape=jax.ShapeDtypeStruct((batch, Lp - ROW_ALIGN, D), F32),
        compiler_params=_params("arbitrary", "arbitrary"),
        name="resnorm_final",
    )(m3, x3, post.reshape(1, 1, D))


def _diff_attn_kernel(lq1_ref, lk1_ref, lq2_ref, lk2_ref, subln_ref, bias_ref,
                      q_ref, k_ref, v_ref, g_ref, o_ref,
                      s0_ref, s1_ref, m_ref, l_ref, acc_ref,
                      *, lambda_init, key_rows, chunk_rows):
    dh = DIFF_DH
    lanes = ROW_ALIGN
    rc = chunk_rows
    tk = key_rows
    n_chunks = q_ref.shape[0] // rc
    n_items = (k_ref.shape[0] // tk) * n_chunks
    n_lane_tiles = tk // lanes
    s_bufs = (s0_ref, s1_ref)

    single_pass = k_ref.shape[0] == tk
    if not single_pass:
        m_ref[...] = jnp.full_like(m_ref, -jnp.inf)
        l_ref[...] = jnp.zeros_like(l_ref)
        acc_ref[...] = jnp.zeros_like(acc_ref)

    def item(t):
        if isinstance(t, int):
            j, c = divmod(t, n_chunks)
            return j, j * tk, c * rc
        j = lax.div(t, n_chunks)
        c = lax.rem(t, n_chunks)
        return j, pl.multiple_of(j * tk, lanes), pl.multiple_of(c * rc, P_STRIP_ROWS)

    def score(t, buf):
        j, k0, q0 = item(t)
        bias = bias_ref[j]
        for mi in range(2):
            qm = q_ref[pl.ds(q0, rc), pl.ds(mi * dh, dh)]
            km = k_ref[pl.ds(k0, tk), pl.ds(mi * dh, dh)]
            s = lax.dot_general(qm, km, (((1,), (1,)), ((), ())), preferred_element_type=F32)
            s_bufs[buf][mi, :, :lanes] = s[:, :lanes] + bias
            if n_lane_tiles > 1:
                s_bufs[buf][mi, :, lanes:] = s[:, lanes:]

    def reduce(t, buf):
        _, k0, start = item(t)
        rows = pl.ds(start, rc)
        v = v_ref[pl.ds(k0, tk), :]
        probs, alphas = [], []
        for mi in range(2):
            m_cur = jnp.max(s_bufs[buf][mi], axis=1, keepdims=True)
            if single_pass:
                m_new = jnp.broadcast_to(m_cur, (rc, lanes))
            else:
                m_prev = m_ref[mi, rows, :]
                m_new = jnp.maximum(m_prev, m_cur)
                alphas.append(jnp.exp2(m_prev - m_new))
                m_ref[mi, rows, :] = m_new
            p = jnp.exp2(s_bufs[buf][mi] - m_new[:, 0:1])
            row_part = p[:, :lanes]
            for ct in range(1, n_lane_tiles):
                row_part = row_part + p[:, ct * lanes:(ct + 1) * lanes]
            if single_pass:
                l_ref[mi, rows, :] = row_part
            else:
                l_ref[mi, rows, :] = alphas[mi] * l_ref[mi, rows, :] + row_part
            probs.append(p.astype(BF16))
        both = jnp.dot(jnp.concatenate(probs, axis=0), v, preferred_element_type=F32)
        for mi in range(2):
            pv = both[mi * rc:(mi + 1) * rc]
            if single_pass:
                acc_ref[mi, rows, :] = pv
            else:
                alpha = alphas[mi]
                acc_ref[mi, rows, :] = acc_ref[mi, rows, :] * jnp.concatenate([alpha, alpha], axis=1) + pv

    score(0, 0)

    def steady(t, carry):
        odd = lax.rem(t, 2) == 1

        @pl.when(jnp.logical_not(odd))
        def _():
            score(t + 1, 1)
            reduce(t, 0)

        @pl.when(odd)
        def _():
            score(t + 1, 0)
            reduce(t, 1)

        return carry

    lax.fori_loop(0, n_items - 1, steady, 0)
    reduce(n_items - 1, (n_items - 1) % 2)

    lam = (jnp.exp(jnp.sum(lq1_ref[...] * lk1_ref[...], axis=1, keepdims=True))
           - jnp.exp(jnp.sum(lq2_ref[...] * lk2_ref[...], axis=1, keepdims=True))
           + lambda_init)
    o1 = acc_ref[0] * (1.0 / jnp.sum(l_ref[0], axis=1, keepdims=True))
    o2 = acc_ref[1] * (lam / jnp.sum(l_ref[1], axis=1, keepdims=True))
    o = o1 - o2
    y = _rms(o, subln_ref[...], SUBLN_EPS) * (1.0 - lambda_init)
    y = y * _silu(g_ref[...].astype(F32))
    o_ref[...] = y.astype(o_ref.dtype)


def _attn_tiling(Lp):
    rc = _divisor_block(Lp, ATTN_CHUNK_ROWS, P_STRIP_ROWS)
    chunks = Lp // rc
    n = max(c for c in range(2, chunks + 1) if chunks % c == 0 and c * rc <= ATTN_BLOCK_Q)
    return n * rc, _divisor_block(Lp, ATTN_BLOCK_K, ROW_ALIGN), rc


def _diff_attention(qk, vg, lam_params, subln, *, batch, lambda_init):
    H = DIFF_HEADS
    R = qk.shape[0]
    Lp = R // batch
    tq, tk, rc = _attn_tiling(Lp)
    nq = Lp // tq
    nk = Lp // tk
    w = 2 * DIFF_DH
    pad_key = (jnp.arange(nk)[:, None, None] == 0) & (jnp.arange(ROW_ALIGN)[None, None, :] < PAD_ROWS)
    bias = jnp.where(pad_key, MASK_BIAS, 0.0).astype(F32)
    vec = pl.BlockSpec((1, DIFF_DH), lambda b, h, i: (0, 0))
    kern = functools.partial(_diff_attn_kernel, lambda_init=lambda_init, key_rows=tk, chunk_rows=rc)
    return pl.pallas_call(
        kern,
        grid=(batch, H, nq),
        in_specs=[
            vec, vec, vec, vec,
            pl.BlockSpec((1, DIFF_DV), lambda b, h, i: (0, 0)),
            pl.BlockSpec((nk, 1, ROW_ALIGN), lambda b, h, i: (0, 0, 0)),
            pl.BlockSpec((tq, w), lambda b, h, i: (b * nq + i, h)),
            pl.BlockSpec((Lp, w), lambda b, h, i: (b, H + h)),
            pl.BlockSpec((Lp, DIFF_DV), lambda b, h, i: (b, h)),
            pl.BlockSpec((tq, DIFF_DV), lambda b, h, i: (b * nq + i, H + h)),
        ],
        out_specs=pl.BlockSpec((tq, DIFF_DV), lambda b, h, i: (b * nq + i, h)),
        out_shape=jax.ShapeDtypeStruct((R, H * DIFF_DV), BF16),
        scratch_shapes=[
            pltpu.VMEM((2, rc, tk), F32),
            pltpu.VMEM((2, rc, tk), F32),
            pltpu.VMEM((2, tq, 128), F32),
            pltpu.VMEM((2, tq, 128), F32),
            pltpu.VMEM((2, tq, DIFF_DV), F32),
        ],
        compiler_params=_params("arbitrary", "arbitrary", "arbitrary"),
        name="diff_attention",
    )(*lam_params, subln, bias, qk, qk, vg, vg)


def _angles(Lp, rot_dim, theta):
    pos = jnp.arange(Lp, dtype=F32) - PAD_ROWS
    half = rot_dim // 2
    inv_freq = jnp.power(jnp.float32(theta), -jnp.arange(half, dtype=F32) * 2.0 / rot_dim)
    return pos[:, None] * inv_freq[None, :]


def _ret_tables(batch, Lp, dk):
    ang = _angles(Lp, dk, RET_THETA)
    tile = lambda t: jnp.tile(t, (batch, 1))
    return tile(jnp.cos(ang)), tile(jnp.sin(ang))


def _diff_tables(batch, Lp):
    ang = _angles(Lp, DIFF_ROT, ROPE_THETA)
    half = DIFF_ROT // 2
    cos, sin = jnp.cos(ang), jnp.sin(ang)
    gap = DIFF_DH // 2 - half
    ones = jnp.ones((Lp, gap), F32)
    zeros = jnp.zeros((Lp, gap), F32)
    c = jnp.concatenate([cos, ones, cos, ones], axis=1)
    s = jnp.concatenate([-sin, zeros, sin, zeros], axis=1)
    tile = lambda t: jnp.tile(t, (batch, 1))
    return tile(c), tile(s)


def _trunk(x, meta_block, pre_norm, post_norm, ret_w, diff_w, ret_decay, diff_small):
    B, S, D = x.shape
    assert S % ROW_ALIGN == 0
    n_chunks = S // ROW_ALIGN + 1
    Lp = n_chunks * ROW_ALIGN
    R = B * Lp
    bm = _divisor_block(R, MM_BLOCK_M, ROW_ALIGN)
    bn = MM_BLOCK_N

    x_cur, h = _embed_norm(x, meta_block, pre_norm[0][None])
    x_cur = x_cur.reshape(R, D)
    h = h.reshape(R, D)

    depth = pre_norm.shape[0]
    for i in range(depth):
        j = i // N_MIXERS
        if i % N_MIXERS == 0:
            w_qk, w_vg, w_out = ret_w[j]
            dk = D // RET_HEADS
            cos, sin = _ret_tables(B, Lp, dk)
            rot = functools.partial(_mm_rot_full_kernel, head_dim=dk,
                                    n_scaled_blocks=(RET_HEADS * dk) // bn, scale=dk ** -0.5)
            qk = _matmul(h, w_qk, bm=bm, bn=bn, out_dtype=BF16, body=rot,
                         row_tables=(cos, sin), name="ret_proj_qk")
            vg = _matmul(h, w_vg, bm=bm, bn=bn, out_dtype=BF16, name="ret_proj_vg")
            raw_f, raw_b = ret_decay[j]
            T = _divisor_block(n_chunks, RET_CHUNKS_PER_STEP, 1)
            o = _retention(qk, vg, raw_f, raw_b, batch=B, n_chunks=n_chunks, chunks_per_step=T)
        else:
            w_qk, w_vg, w_out = diff_w[j]
            lam_params, subln = diff_small[j]
            lambda_init = 0.8 - 0.6 * math.exp(-0.3 * i)
            tabs = _diff_tables(B, Lp)
            rot = functools.partial(_mm_rot_part_kernel, group=DIFF_DH,
                                    n_scaled_blocks=(DIFF_HEADS * 2 * DIFF_DH) // bn,
                                    scale=DIFF_DH ** -0.5 * math.log2(math.e))
            qk = _matmul(h, w_qk, bm=bm, bn=bn, out_dtype=BF16, body=rot,
                         row_tables=tabs, name="diff_proj_qk")
            vg = _matmul(h, w_vg, bm=bm, bn=bn, out_dtype=BF16, name="diff_proj_vg")
            o = _diff_attention(qk, vg, lam_params, subln, batch=B, lambda_init=lambda_init)
        m = _matmul_ksplit(o, w_out, bm=bm, bn=bn, bk=o.shape[1] // 2, name="out_proj")
        if i + 1 < depth:
            x_cur, h = _resnorm(m, x_cur, post_norm[i][None], pre_norm[i + 1][None],
                                bm=_divisor_block(R, RESNORM_BLOCK_M, ROW_ALIGN))
        else:
            return _resnorm_final(m, x_cur, post_norm[i], batch=B)


def kernel(x_prompt, x_sample, meta_tokens, pre_norm, post_norm, ret_w_in, ret_w_out, ret_decay_fwd,
           ret_decay_bwd, diff_w_in, diff_w_out, diff_lambda_q1, diff_lambda_k1, diff_lambda_q2,
           diff_lambda_k2, diff_subln):
    D = x_prompt.shape[-1]
    meta_block = jnp.pad(meta_tokens.astype(F32), ((PAD_ROWS, 0), (0, 0)))

    ret_qk_cols = 2 * D
    ret_w = [(ret_w_in[j][:, :ret_qk_cols].astype(BF16), ret_w_in[j][:, ret_qk_cols:].astype(BF16),
              ret_w_out[j].astype(BF16)) for j in range(ret_w_in.shape[0])]
    diff_qk_cols = 2 * DIFF_HEADS * 2 * DIFF_DH
    diff_w = [(_rotary_half_swap(diff_w_in[j][:, :diff_qk_cols], DIFF_DH, DIFF_ROT).astype(BF16),
               diff_w_in[j][:, diff_qk_cols:].astype(BF16),
               diff_w_out[j].astype(BF16)) for j in range(diff_w_in.shape[0])]
    bcast = lambda v: jnp.broadcast_to(v.astype(F32)[:, None, None], (v.shape[0], 8, 128))
    ret_decay = [(bcast(ret_decay_fwd[j]), bcast(ret_decay_bwd[j])) for j in range(ret_decay_fwd.shape[0])]
    diff_small = [((diff_lambda_q1[j][None], diff_lambda_k1[j][None], diff_lambda_q2[j][None],
                    diff_lambda_k2[j][None]), diff_subln[j][None]) for j in range(diff_subln.shape[0])]

    run = lambda x: _trunk(x, meta_block, pre_norm, post_norm, ret_w, diff_w, ret_decay, diff_small)
    return (run(x_prompt), run(x_sample))
```

```python
import functools
import math

import jax
import jax.numpy as jnp
from jax import lax
from jax.experimental import pallas as pl
from jax.experimental.pallas import tpu as pltpu

N_META = 16
N_MIXERS = 2
RET_HEADS = 16
RET_CHUNK = 128
RET_THETA = 10000.0
DIFF_HEADS = 32
DIFF_DH = 128
DIFF_DV = 2 * DIFF_DH
DIFF_ROT = DIFF_DH // 4
ROPE_THETA = 500000.0
NORM_EPS = 1e-6
SUBLN_EPS = 1e-5

VREG_SUBLANES, VREG_LANES = 8, 128
ROW_ALIGN = 128
PAD_ROWS = ROW_ALIGN - N_META
P_STRIP_ROWS = 16
MASK_BIAS = -1e30
V7X_VMEM_LIMIT_BYTES = 56 * 1024 * 1024

MM_BLOCK_M = 1024
MM_BLOCK_N = 1024
RESNORM_BLOCK_M = 256
ATTN_BLOCK_Q = 2176
ATTN_BLOCK_K = 2176
ATTN_CHUNK_ROWS = 544
ATTN_PAIRED_MAX_ITEMS = 8
RET_CHUNKS_PER_STEP = 17

F32 = jnp.float32
BF16 = jnp.bfloat16


def _divisor_block(total, target, mult):
    best = None
    for cand in range(mult, min(total, target) + 1, mult):
        if total % cand == 0:
            best = cand
    assert best is not None, (total, target, mult)
    return best


def _params(*sem, flags=None):
    return pltpu.CompilerParams(dimension_semantics=sem, vmem_limit_bytes=V7X_VMEM_LIMIT_BYTES,
                                flags=flags)


def _rms(xf, gain, eps):
    y = xf * lax.rsqrt(jnp.mean(xf * xf, axis=-1, keepdims=True) + eps)
    if gain is not None:
        y = y * gain
    return y


def _silu(g):
    return g * (1.0 / (1.0 + jnp.exp(-g)))


def _embed_norm_kernel(x_ref, meta_ref, gain_ref, x0_ref, h_ref):
    t = pl.program_id(1)
    xb = jnp.where(t == 0, meta_ref[...], x_ref[0])
    x0_ref[0] = xb
    h_ref[0] = _rms(xb, gain_ref[...], NORM_EPS).astype(BF16)


def _embed_norm(x, meta_block, gain):
    B, S, D = x.shape
    n_blk = S // ROW_ALIGN + 1
    Lp = n_blk * ROW_ALIGN
    blk = (1, ROW_ALIGN, D)
    return pl.pallas_call(
        _embed_norm_kernel,
        grid=(B, n_blk),
        in_specs=[
            pl.BlockSpec(blk, lambda b, t: (b, jnp.maximum(t - 1, 0), 0)),
            pl.BlockSpec((ROW_ALIGN, D), lambda b, t: (0, 0)),
            pl.BlockSpec((1, D), lambda b, t: (0, 0)),
        ],
        out_specs=[pl.BlockSpec(blk, lambda b, t: (b, t, 0)),
                   pl.BlockSpec(blk, lambda b, t: (b, t, 0))],
        out_shape=[jax.ShapeDtypeStruct((B, Lp, D), F32),
                   jax.ShapeDtypeStruct((B, Lp, D), BF16)],
        compiler_params=_params("arbitrary", "arbitrary"),
        name="embed_norm",
    )(x, meta_block, gain)


def _mm_plain_kernel(x_ref, w_ref, o_ref):
    o_ref[...] = jnp.dot(x_ref[...], w_ref[...], preferred_element_type=F32).astype(o_ref.dtype)


def _mm_rot_full_kernel(x_ref, w_ref, cos_ref, sin_ref, o_ref, *, head_dim, n_scaled_blocks, scale):
    acc = jnp.dot(x_ref[...], w_ref[...], preferred_element_type=F32)
    sc = jnp.where(pl.program_id(1) < n_scaled_blocks, scale, 1.0).astype(F32)
    c = cos_ref[...] * sc
    s = sin_ref[...] * sc
    half = head_dim // 2
    for h0 in range(0, acc.shape[1], head_dim):
        x1 = acc[:, h0:h0 + half]
        x2 = acc[:, h0 + half:h0 + head_dim]
        o_ref[:, h0:h0 + half] = (x1 * c - x2 * s).astype(o_ref.dtype)
        o_ref[:, h0 + half:h0 + head_dim] = (x2 * c + x1 * s).astype(o_ref.dtype)


def _mm_rot_part_kernel(x_ref, w_ref, c_ref, s_ref, o_ref, *, group, n_scaled_blocks, scale):
    acc = jnp.dot(x_ref[...], w_ref[...], preferred_element_type=F32)
    sc = jnp.where(pl.program_id(1) < n_scaled_blocks, scale, 1.0).astype(F32)
    c = c_ref[...] * sc
    s = s_ref[...] * sc
    for g0 in range(0, acc.shape[1], group):
        xg = acc[:, g0:g0 + group]
        partner = pltpu.roll(xg, group // 2, 1)
        o_ref[:, g0:g0 + group] = (xg * c + partner * s).astype(o_ref.dtype)


def _rotary_half_swap(w, group, rot_dim):
    half = rot_dim // 2
    k_in = w.shape[0]
    g = w.reshape(k_in, -1, group)
    g = jnp.concatenate([g[..., :half], g[..., rot_dim:group // 2 + half], g[..., half:rot_dim],
                         g[..., group // 2 + half:]], axis=-1)
    return g.reshape(k_in, -1)


def _matmul(x, w, *, bm, bn, out_dtype, body=_mm_plain_kernel, row_tables=(), name):
    M, K = x.shape
    N = w.shape[1]
    tab_specs = [pl.BlockSpec((bm, t.shape[1]), lambda i, j: (i, 0)) for t in row_tables]
    return pl.pallas_call(
        body,
        grid=(M // bm, N // bn),
        in_specs=[pl.BlockSpec((bm, K), lambda i, j: (i, 0)),
                  pl.BlockSpec((K, bn), lambda i, j: (0, j))] + tab_specs,
        out_specs=pl.BlockSpec((bm, bn), lambda i, j: (i, j)),
        out_shape=jax.ShapeDtypeStruct((M, N), out_dtype),
        compiler_params=_params("arbitrary", "arbitrary"),
        name=name,
    )(x, w, *row_tables)


def _mm_acc_kernel(x_ref, w_ref, o_ref):
    part = jnp.dot(x_ref[...], w_ref[...], preferred_element_type=F32)
    k = pl.program_id(2)

    @pl.when(k == 0)
    def _():
        o_ref[...] = part

    @pl.when(k != 0)
    def _():
        o_ref[...] += part


def _matmul_ksplit(x, w, *, bm, bn, bk, name):
    M, K = x.shape
    N = w.shape[1]
    return pl.pallas_call(
        _mm_acc_kernel,
        grid=(M // bm, N // bn, K // bk),
        in_specs=[pl.BlockSpec((bm, bk), lambda i, j, k: (i, k)),
                  pl.BlockSpec((bk, bn), lambda i, j, k: (k, j))],
        out_specs=pl.BlockSpec((bm, bn), lambda i, j, k: (i, j)),
        out_shape=jax.ShapeDtypeStruct((M, N), F32),
        compiler_params=_params("arbitrary", "arbitrary", "arbitrary"),
        name=name,
    )(x, w)


def _retention_kernel(rawf_ref, rawb_ref, q_ref, k_ref, v_ref, g_ref, o_ref,
                      state_ref, stash_ref, dmat_ref, qdec_ref, kdec_ref, cdec_ref,
                      *, chunks_per_step, n_steps):
    C = RET_CHUNK
    T = chunks_per_step
    phase = pl.program_id(2)
    step = pl.program_id(3)
    dk = q_ref.shape[1]
    dv = v_ref.shape[1]

    @pl.when((phase == 0) & (step == 0))
    def _():
        lgf = jnp.broadcast_to(-jnp.exp(rawf_ref[0])[0:1, :], (C, C))
        lgb = jnp.broadcast_to(-jnp.exp(rawb_ref[0])[0:1, :], (C, C))
        row = lax.broadcasted_iota(jnp.int32, (C, C), 0).astype(F32)
        col = lax.broadcasted_iota(jnp.int32, (C, C), 1).astype(F32)
        rel = row - col
        d_f = jnp.where(rel >= 0, jnp.exp(jnp.where(rel >= 0, rel, 0.0) * lgf), 0.0)
        d_b = jnp.where(rel < 0, jnp.exp(jnp.where(rel < 0, -rel, 0.0) * lgb), 0.0)
        dmat_ref[...] = d_f + d_b
        reps_k = dk // C
        qdec_ref[0] = jnp.concatenate([jnp.exp((row + 1.0) * lgf)] * reps_k, axis=1)
        kdec_ref[0] = jnp.concatenate([jnp.exp((C - 1.0 - row) * lgf)] * reps_k, axis=1)
        qdec_ref[1] = jnp.concatenate([jnp.exp((C - row) * lgb)] * reps_k, axis=1)
        kdec_ref[1] = jnp.concatenate([jnp.exp(row * lgb)] * reps_k, axis=1)
        reps_v = dv // C
        cdec_ref[0] = jnp.concatenate([jnp.exp(C * lgf[0:8, :])] * reps_v, axis=1)
        cdec_ref[1] = jnp.concatenate([jnp.exp(C * lgb[0:8, :])] * reps_v, axis=1)

    @pl.when(step == 0)
    def _():
        state_ref[...] = jnp.zeros_like(state_ref)

    def cross_and_update(direction, rows):
        qc = q_ref[rows, :]
        kc = k_ref[rows, :]
        vc = v_ref[rows, :]
        st = state_ref[...]
        qd = (qc.astype(F32) * qdec_ref[direction]).astype(BF16)
        cross = jnp.dot(qd, st.astype(BF16), preferred_element_type=F32)
        kd = (kc.astype(F32) * kdec_ref[direction]).astype(BF16)
        upd = lax.dot_general(kd, vc, (((0,), (0,)), ((), ())), preferred_element_type=F32)
        state_ref[...] = st * cdec_ref[direction][0:1, :] + upd
        return qc, kc, vc, cross

    @pl.when(phase == 0)
    def _():
        for c in range(T):
            rows = pl.ds(c * C, C)
            qc, kc, vc, cross = cross_and_update(0, rows)
            a = lax.dot_general(qc, kc, (((1,), (1,)), ((), ())), preferred_element_type=F32)
            pm = (a * dmat_ref[...]).astype(BF16)
            intra = jnp.dot(pm, vc, preferred_element_type=F32)
            stash_ref[step * T + c] = intra + cross

    @pl.when(phase == 1)
    def _():
        blk = n_steps - 1 - step
        for c in range(T - 1, -1, -1):
            rows = pl.ds(c * C, C)
            _, _, _, cross = cross_and_update(1, rows)
            o = stash_ref[blk * T + c] + cross
            y = _rms(o, None, NORM_EPS)
            y = y * _silu(g_ref[rows, :].astype(F32))
            o_ref[rows, :] = y.astype(o_ref.dtype)


def _retention(qk, vg, raw_f, raw_b, *, batch, n_chunks, chunks_per_step):
    H = RET_HEADS
    R = qk.shape[0]
    dk = qk.shape[1] // (2 * H)
    dv = vg.shape[1] // (2 * H)
    T = chunks_per_step
    NS = n_chunks // T
    rows = T * RET_CHUNK

    def sweep_block(b, p, s):
        return b * NS + jnp.where(p == 0, s, NS - 1 - s)

    def store_block(b, p, s):
        return b * NS + jnp.where(p == 0, NS - 1, NS - 1 - s)

    kern = functools.partial(_retention_kernel, chunks_per_step=T, n_steps=NS)
    return pl.pallas_call(
        kern,
        grid=(batch, H, 2, NS),
        in_specs=[
            pl.BlockSpec((1, VREG_SUBLANES, VREG_LANES), lambda b, h, p, s: (h, 0, 0)),
            pl.BlockSpec((1, VREG_SUBLANES, VREG_LANES), lambda b, h, p, s: (h, 0, 0)),
            pl.BlockSpec((rows, dk), lambda b, h, p, s: (sweep_block(b, p, s), h)),
            pl.BlockSpec((rows, dk), lambda b, h, p, s: (sweep_block(b, p, s), H + h)),
            pl.BlockSpec((rows, dv), lambda b, h, p, s: (sweep_block(b, p, s), h)),
            pl.BlockSpec((rows, dv), lambda b, h, p, s: (store_block(b, p, s), H + h)),
        ],
        out_specs=pl.BlockSpec((rows, dv), lambda b, h, p, s: (store_block(b, p, s), h)),
        out_shape=jax.ShapeDtypeStruct((R, H * dv), BF16),
        scratch_shapes=[
            pltpu.VMEM((dk, dv), F32),
            pltpu.VMEM((n_chunks, RET_CHUNK, dv), F32),
            pltpu.VMEM((RET_CHUNK, RET_CHUNK), F32),
            pltpu.VMEM((2, RET_CHUNK, dk), F32),
            pltpu.VMEM((2, RET_CHUNK, dk), F32),
            pltpu.VMEM((2, 8, dv), F32),
        ],
        compiler_params=_params("arbitrary", "arbitrary", "arbitrary", "arbitrary"),
        name="retention",
    )(raw_f, raw_b, qk, qk, vg, vg)


def _resnorm_kernel(m_ref, x_ref, post_ref, pre_ref, xo_ref, h_ref):
    xn = x_ref[...] + _rms(m_ref[...], post_ref[...], NORM_EPS)
    xo_ref[...] = xn
    h_ref[...] = _rms(xn, pre_ref[...], NORM_EPS).astype(BF16)


def _resnorm(m, x, post, pre, *, bm):
    R, D = m.shape
    row = pl.BlockSpec((bm, D), lambda i: (i, 0))
    vec = pl.BlockSpec((1, D), lambda i: (0, 0))
    return pl.pallas_call(
        _resnorm_kernel,
        grid=(R // bm,),
        in_specs=[row, row, vec, vec],
        out_specs=[row, row],
        out_shape=[jax.ShapeDtypeStruct((R, D), F32), jax.ShapeDtypeStruct((R, D), BF16)],
        compiler_params=_params("arbitrary"),
        name="resnorm",
    )(m, x, post, pre)


def _resnorm_final_kernel(m_ref, x_ref, post_ref, y_ref):
    y_ref[...] = x_ref[...] + _rms(m_ref[...], post_ref[...], NORM_EPS)


def _resnorm_final(m, x, post, *, batch):
    R, D = m.shape
    Lp = R // batch
    n_blk = Lp // ROW_ALIGN
    m3 = m.reshape(batch, Lp, D)
    x3 = x.reshape(batch, Lp, D)
    blk = (1, ROW_ALIGN, D)
    return pl.pallas_call(
        _resnorm_final_kernel,
        grid=(batch, n_blk - 1),
        in_specs=[pl.BlockSpec(blk, lambda b, t: (b, t + 1, 0)),
                  pl.BlockSpec(blk, lambda b, t: (b, t + 1, 0)),
                  pl.BlockSpec((1, 1, D), lambda b, t: (0, 0, 0))],
        out_specs=pl.BlockSpec(blk, lambda b, t: (b, t, 0)),
        out_shape=jax.ShapeDtypeStruct((batch, Lp - ROW_ALIGN, D), F32),
        compiler_params=_params("arbitrary", "arbitrary"),
        name="resnorm_final",
    )(m3, x3, post.reshape(1, 1, D))


def _diff_attn_kernel(lq1_ref, lk1_ref, lq2_ref, lk2_ref, subln_ref, bias_ref,
                      q_ref, k_ref, v_ref, g_ref, o_ref,
                      s0_ref, s1_ref, m_ref, l_ref, acc_ref,
                      *, lambda_init, key_rows, chunk_rows):
    dh = DIFF_DH
    lanes = ROW_ALIGN
    rc = chunk_rows
    tk = key_rows
    n_chunks = q_ref.shape[0] // rc
    n_items = (k_ref.shape[0] // tk) * n_chunks
    n_lane_tiles = tk // lanes
    s_bufs = (s0_ref, s1_ref)

    single_pass = k_ref.shape[0] == tk
    if not single_pass:
        m_ref[...] = jnp.full_like(m_ref, -jnp.inf)
        l_ref[...] = jnp.zeros_like(l_ref)
        acc_ref[...] = jnp.zeros_like(acc_ref)

    def item(t):
        if isinstance(t, int):
            j, c = divmod(t, n_chunks)
            return j, j * tk, c * rc
        j = lax.div(t, n_chunks)
        c = lax.rem(t, n_chunks)
        return j, pl.multiple_of(j * tk, lanes), pl.multiple_of(c * rc, P_STRIP_ROWS)

    def score(t, buf):
        j, k0, q0 = item(t)
        bias = bias_ref[j]
        for mi in range(2):
            qm = q_ref[pl.ds(q0, rc), pl.ds(mi * dh, dh)]
            km = k_ref[pl.ds(k0, tk), pl.ds(mi * dh, dh)]
            s = lax.dot_general(qm, km, (((1,), (1,)), ((), ())), preferred_element_type=F32)
            s_bufs[buf][mi, :, :lanes] = s[:, :lanes] + bias
            if n_lane_tiles > 1:
                s_bufs[buf][mi, :, lanes:] = s[:, lanes:]

    def reduce(t, buf):
        _, k0, start = item(t)
        rows = pl.ds(start, rc)
        v = v_ref[pl.ds(k0, tk), :]
        probs, alphas = [], []
        for mi in range(2):
            m_cur = jnp.max(s_bufs[buf][mi], axis=1, keepdims=True)
            if single_pass:
                m_new = jnp.broadcast_to(m_cur, (rc, lanes))
            else:
                m_prev = m_ref[mi, rows, :]
                m_new = jnp.maximum(m_prev, m_cur)
                alphas.append(jnp.exp2(m_prev - m_new))
                m_ref[mi, rows, :] = m_new
            p = jnp.exp2(s_bufs[buf][mi] - m_new[:, 0:1])
            row_part = p[:, :lanes]
            for ct in range(1, n_lane_tiles):
                row_part = row_part + p[:, ct * lanes:(ct + 1) * lanes]
            if single_pass:
                l_ref[mi, rows, :] = row_part
            else:
                l_ref[mi, rows, :] = alphas[mi] * l_ref[mi, rows, :] + row_part
            probs.append(p.astype(BF16))
        both = jnp.dot(jnp.concatenate(probs, axis=0), v, preferred_element_type=F32)
        for mi in range(2):
            pv = both[mi * rc:(mi + 1) * rc]
            if single_pass:
                acc_ref[mi, rows, :] = pv
            else:
                alpha = alphas[mi]
                acc_ref[mi, rows, :] = acc_ref[mi, rows, :] * jnp.concatenate([alpha, alpha], axis=1) + pv

    score(0, 0)

    if n_items <= ATTN_PAIRED_MAX_ITEMS:
        def steady_pair(i, carry):
            t = 2 * i
            score(t + 1, 1)
            reduce(t, 0)
            score(t + 2, 0)
            reduce(t + 1, 1)
            return carry

        n_pairs = (n_items - 1) // 2
        lax.fori_loop(0, n_pairs, steady_pair, 0)
        if 2 * n_pairs < n_items - 1:
            score(n_items - 1, 1)
            reduce(n_items - 2, 0)
    else:
        def steady(t, carry):
            odd = lax.rem(t, 2) == 1

            @pl.when(jnp.logical_not(odd))
            def _():
                score(t + 1, 1)
                reduce(t, 0)

            @pl.when(odd)
            def _():
                score(t + 1, 0)
                reduce(t, 1)

            return carry

        lax.fori_loop(0, n_items - 1, steady, 0)
    reduce(n_items - 1, (n_items - 1) % 2)

    lam = (jnp.exp(jnp.sum(lq1_ref[...] * lk1_ref[...], axis=1, keepdims=True))
           - jnp.exp(jnp.sum(lq2_ref[...] * lk2_ref[...], axis=1, keepdims=True))
           + lambda_init)
    o1 = acc_ref[0] * (1.0 / jnp.sum(l_ref[0], axis=1, keepdims=True))
    o2 = acc_ref[1] * (lam / jnp.sum(l_ref[1], axis=1, keepdims=True))
    o = o1 - o2
    y = _rms(o, subln_ref[...], SUBLN_EPS) * (1.0 - lambda_init)
    y = y * _silu(g_ref[...].astype(F32))
    o_ref[...] = y.astype(o_ref.dtype)


def _attn_tiling(Lp):
    rc = _divisor_block(Lp, ATTN_CHUNK_ROWS, P_STRIP_ROWS)
    chunks = Lp // rc
    n = max(c for c in range(2, chunks + 1) if chunks % c == 0 and c * rc <= ATTN_BLOCK_Q)
    return n * rc, _divisor_block(Lp, ATTN_BLOCK_K, ROW_ALIGN), rc


def _diff_attention(qk, vg, lam_params, subln, *, batch, lambda_init):
    H = DIFF_HEADS
    R = qk.shape[0]
    Lp = R // batch
    tq, tk, rc = _attn_tiling(Lp)
    nq = Lp // tq
    nk = Lp // tk
    w = 2 * DIFF_DH
    pad_key = (jnp.arange(nk)[:, None, None] == 0) & (jnp.arange(ROW_ALIGN)[None, None, :] < PAD_ROWS)
    bias = jnp.where(pad_key, MASK_BIAS, 0.0).astype(F32)
    vec = pl.BlockSpec((1, DIFF_DH), lambda b, h, i: (0, 0))
    kern = functools.partial(_diff_attn_kernel, lambda_init=lambda_init, key_rows=tk, chunk_rows=rc)
    return pl.pallas_call(
        kern,
        grid=(batch, H, nq),
        in_specs=[
            vec, vec, vec, vec,
            pl.BlockSpec((1, DIFF_DV), lambda b, h, i: (0, 0)),
            pl.BlockSpec((nk, 1, ROW_ALIGN), lambda b, h, i: (0, 0, 0)),
            pl.BlockSpec((tq, w), lambda b, h, i: (b * nq + i, h)),
            pl.BlockSpec((Lp, w), lambda b, h, i: (b, H + h)),
            pl.BlockSpec((Lp, DIFF_DV), lambda b, h, i: (b, h)),
            pl.BlockSpec((tq, DIFF_DV), lambda b, h, i: (b * nq + i, H + h)),
        ],
        out_specs=pl.BlockSpec((tq, DIFF_DV), lambda b, h, i: (b * nq + i, h)),
        out_shape=jax.ShapeDtypeStruct((R, H * DIFF_DV), BF16),
        scratch_shapes=[
            pltpu.VMEM((2, rc, tk), F32),
            pltpu.VMEM((2, rc, tk), F32),
            pltpu.VMEM((2, tq, VREG_LANES), F32),
            pltpu.VMEM((2, tq, VREG_LANES), F32),
            pltpu.VMEM((2, tq, DIFF_DV), F32),
        ],
        compiler_params=_params("arbitrary", "arbitrary", "arbitrary"),
        name="diff_attention",
    )(*lam_params, subln, bias, qk, qk, vg, vg)


def _angles(Lp, rot_dim, theta):
    pos = jnp.arange(Lp, dtype=F32) - PAD_ROWS
    half = rot_dim // 2
    inv_freq = jnp.power(jnp.float32(theta), -jnp.arange(half, dtype=F32) * 2.0 / rot_dim)
    return pos[:, None] * inv_freq[None, :]


def _ret_tables(batch, Lp, dk):
    ang = _angles(Lp, dk, RET_THETA)
    tile = lambda t: jnp.tile(t, (batch, 1))
    return tile(jnp.cos(ang)), tile(jnp.sin(ang))


def _diff_tables(batch, Lp):
    ang = _angles(Lp, DIFF_ROT, ROPE_THETA)
    half = DIFF_ROT // 2
    cos, sin = jnp.cos(ang), jnp.sin(ang)
    gap = DIFF_DH // 2 - half
    ones = jnp.ones((Lp, gap), F32)
    zeros = jnp.zeros((Lp, gap), F32)
    c = jnp.concatenate([cos, ones, cos, ones], axis=1)
    s = jnp.concatenate([-sin, zeros, sin, zeros], axis=1)
    tile = lambda t: jnp.tile(t, (batch, 1))
    return tile(c), tile(s)


def _trunk(x, meta_block, pre_norm, post_norm, ret_w, diff_w, ret_decay, diff_small):
    B, S, D = x.shape
    assert S % ROW_ALIGN == 0
    n_chunks = S // ROW_ALIGN + 1
    Lp = n_chunks * ROW_ALIGN
    R = B * Lp
    bm = _divisor_block(R, MM_BLOCK_M, ROW_ALIGN)
    bn = MM_BLOCK_N

    x_cur, h = _embed_norm(x, meta_block, pre_norm[0][None])
    x_cur = x_cur.reshape(R, D)
    h = h.reshape(R, D)

    depth = pre_norm.shape[0]
    for i in range(depth):
        j = i // N_MIXERS
        if i % N_MIXERS == 0:
            w_qk, w_vg, w_out = ret_w[j]
            dk = D // RET_HEADS
            cos, sin = _ret_tables(B, Lp, dk)
            rot = functools.partial(_mm_rot_full_kernel, head_dim=dk,
                                    n_scaled_blocks=(RET_HEADS * dk) // bn, scale=dk ** -0.5)
            qk = _matmul(h, w_qk, bm=bm, bn=bn, out_dtype=BF16, body=rot,
                         row_tables=(cos, sin), name="ret_proj_qk")
            vg = _matmul(h, w_vg, bm=bm, bn=bn, out_dtype=BF16, name="ret_proj_vg")
            raw_f, raw_b = ret_decay[j]
            T = _divisor_block(n_chunks, RET_CHUNKS_PER_STEP, 1)
            o = _retention(qk, vg, raw_f, raw_b, batch=B, n_chunks=n_chunks, chunks_per_step=T)
        else:
            w_qk, w_vg, w_out = diff_w[j]
            lam_params, subln = diff_small[j]
            lambda_init = 0.8 - 0.6 * math.exp(-0.3 * i)
            tabs = _diff_tables(B, Lp)
            rot = functools.partial(_mm_rot_part_kernel, group=DIFF_DH,
                                    n_scaled_blocks=(DIFF_HEADS * 2 * DIFF_DH) // bn,
                                    scale=DIFF_DH ** -0.5 * math.log2(math.e))
            qk = _matmul(h, w_qk, bm=bm, bn=bn, out_dtype=BF16, body=rot,
                         row_tables=tabs, name="diff_proj_qk")
            vg = _matmul(h, w_vg, bm=bm, bn=bn, out_dtype=BF16, name="diff_proj_vg")
            o = _diff_attention(qk, vg, lam_params, subln, batch=B, lambda_init=lambda_init)
        m = _matmul_ksplit(o, w_out, bm=bm, bn=bn, bk=o.shape[1] // 2, name="out_proj")
        if i + 1 < depth:
            x_cur, h = _resnorm(m, x_cur, post_norm[i][None], pre_norm[i + 1][None],
                                bm=_divisor_block(R, RESNORM_BLOCK_M, ROW_ALIGN))
        else:
            return _resnorm_final(m, x_cur, post_norm[i], batch=B)


def kernel(x_prompt, x_sample, meta_tokens, pre_norm, post_norm, ret_w_in, ret_w_out, ret_decay_fwd,
           ret_decay_bwd, diff_w_in, diff_w_out, diff_lambda_q1, diff_lambda_k1, diff_lambda_q2,
           diff_lambda_k2, diff_subln):
    D = x_prompt.shape[-1]
    meta_block = jnp.pad(meta_tokens.astype(F32), ((PAD_ROWS, 0), (0, 0)))

    ret_qk_cols = 2 * D
    ret_w = [(ret_w_in[j][:, :ret_qk_cols].astype(BF16), ret_w_in[j][:, ret_qk_cols:].astype(BF16),
              ret_w_out[j].astype(BF16)) for j in range(ret_w_in.shape[0])]
    diff_qk_cols = 2 * DIFF_HEADS * 2 * DIFF_DH
    diff_w = [(_rotary_half_swap(diff_w_in[j][:, :diff_qk_cols], DIFF_DH, DIFF_ROT).astype(BF16),
               diff_w_in[j][:, diff_qk_cols:].astype(BF16),
               diff_w_out[j].astype(BF16)) for j in range(diff_w_in.shape[0])]
    bcast = lambda v: jnp.broadcast_to(v.astype(F32)[:, None, None],
                                       (v.shape[0], VREG_SUBLANES, VREG_LANES))
    ret_decay = [(bcast(ret_decay_fwd[j]), bcast(ret_decay_bwd[j])) for j in range(ret_decay_fwd.shape[0])]
    diff_small = [((diff_lambda_q1[j][None], diff_lambda_k1[j][None], diff_lambda_q2[j][None],
                    diff_lambda_k2[j][None]), diff_subln[j][None]) for j in range(diff_subln.shape[0])]

    run = lambda x: _trunk(x, meta_block, pre_norm, post_norm, ret_w, diff_w, ret_decay, diff_small)
    return (run(x_prompt), run(x_sample))
```

```python
import functools
import math

import jax
import jax.numpy as jnp
from jax import lax
from jax.experimental import pallas as pl
from jax.experimental.pallas import tpu as pltpu

N_META = 16
N_MIXERS = 2
RET_HEADS = 16
RET_CHUNK = 128
RET_THETA = 10000.0
DIFF_HEADS = 32
DIFF_DH = 128
DIFF_DV = 2 * DIFF_DH
DIFF_ROT = DIFF_DH // 4
ROPE_THETA = 500000.0
NORM_EPS = 1e-6
SUBLN_EPS = 1e-5

ROW_ALIGN = 128
PAD_ROWS = ROW_ALIGN - N_META
P_STRIP_ROWS = 16
MASK_BIAS = -1e30
V7X_VMEM_LIMIT_BYTES = 56 * 1024 * 1024

MM_BLOCK_M = 1024
MM_BLOCK_N = 1024
RESNORM_BLOCK_M = 256
ATTN_BLOCK_Q = 2176
ATTN_BLOCK_K = 2176
ATTN_CHUNK_ROWS = 544
RET_CHUNKS_PER_STEP = 17

F32 = jnp.float32
BF16 = jnp.bfloat16


def _divisor_block(total, target, mult):
    best = None
    for cand in range(mult, min(total, target) + 1, mult):
        if total % cand == 0:
            best = cand
    assert best is not None, (total, target, mult)
    return best


def _params(*sem, flags=None):
    return pltpu.CompilerParams(dimension_semantics=sem, vmem_limit_bytes=V7X_VMEM_LIMIT_BYTES,
                                flags=flags)


def _rms(xf, gain, eps):
    y = xf * lax.rsqrt(jnp.mean(xf * xf, axis=-1, keepdims=True) + eps)
    if gain is not None:
        y = y * gain
    return y


def _silu(g):
    return g * (1.0 / (1.0 + jnp.exp(-g)))


def _embed_norm_kernel(x_ref, meta_ref, gain_ref, x0_ref, h_ref):
    t = pl.program_id(1)
    xb = jnp.where(t == 0, meta_ref[...], x_ref[0])
    x0_ref[0] = xb
    h_ref[0] = _rms(xb, gain_ref[...], NORM_EPS).astype(BF16)


def _embed_norm(x, meta_block, gain):
    B, S, D = x.shape
    n_blk = S // ROW_ALIGN + 1
    Lp = n_blk * ROW_ALIGN
    blk = (1, ROW_ALIGN, D)
    return pl.pallas_call(
        _embed_norm_kernel,
        grid=(B, n_blk),
        in_specs=[
            pl.BlockSpec(blk, lambda b, t: (b, jnp.maximum(t - 1, 0), 0)),
            pl.BlockSpec((ROW_ALIGN, D), lambda b, t: (0, 0)),
            pl.BlockSpec((1, D), lambda b, t: (0, 0)),
        ],
        out_specs=[pl.BlockSpec(blk, lambda b, t: (b, t, 0)),
                   pl.BlockSpec(blk, lambda b, t: (b, t, 0))],
        out_shape=[jax.ShapeDtypeStruct((B, Lp, D), F32),
                   jax.ShapeDtypeStruct((B, Lp, D), BF16)],
        compiler_params=_params("arbitrary", "arbitrary"),
        name="embed_norm",
    )(x, meta_block, gain)


def _mm_plain_kernel(x_ref, w_ref, o_ref):
    o_ref[...] = jnp.dot(x_ref[...], w_ref[...], preferred_element_type=F32).astype(o_ref.dtype)


def _mm_rot_full_kernel(x_ref, w_ref, cos_ref, sin_ref, o_ref, *, head_dim, n_scaled_blocks, scale):
    acc = jnp.dot(x_ref[...], w_ref[...], preferred_element_type=F32)
    sc = jnp.where(pl.program_id(1) < n_scaled_blocks, scale, 1.0).astype(F32)
    c = cos_ref[...] * sc
    s = sin_ref[...] * sc
    half = head_dim // 2
    for h0 in range(0, acc.shape[1], head_dim):
        x1 = acc[:, h0:h0 + half]
        x2 = acc[:, h0 + half:h0 + head_dim]
        o_ref[:, h0:h0 + half] = (x1 * c - x2 * s).astype(o_ref.dtype)
        o_ref[:, h0 + half:h0 + head_dim] = (x2 * c + x1 * s).astype(o_ref.dtype)


def _mm_rot_part_kernel(x_ref, w_ref, c_ref, s_up_ref, s_dn_ref, o_ref, *, group, rot_half,
                        n_scaled_blocks, scale):
    acc = jnp.dot(x_ref[...], w_ref[...], preferred_element_type=F32)
    sc = jnp.where(pl.program_id(1) < n_scaled_blocks, scale, 1.0).astype(F32)
    c = c_ref[...] * sc
    s_up = s_up_ref[...] * sc
    s_dn = s_dn_ref[...] * sc
    for g0 in range(0, acc.shape[1], group):
        xg = acc[:, g0:g0 + group]
        from_lo = pltpu.roll(xg, rot_half, 1)
        from_hi = pltpu.roll(xg, group - rot_half, 1)
        o_ref[:, g0:g0 + group] = (xg * c + from_lo * s_up + from_hi * s_dn).astype(o_ref.dtype)


def _matmul(x, w, *, bm, bn, out_dtype, body=_mm_plain_kernel, row_tables=(), name):
    M, K = x.shape
    N = w.shape[1]
    tab_specs = [pl.BlockSpec((bm, t.shape[1]), lambda i, j: (i, 0)) for t in row_tables]
    return pl.pallas_call(
        body,
        grid=(M // bm, N // bn),
        in_specs=[pl.BlockSpec((bm, K), lambda i, j: (i, 0)),
                  pl.BlockSpec((K, bn), lambda i, j: (0, j))] + tab_specs,
        out_specs=pl.BlockSpec((bm, bn), lambda i, j: (i, j)),
        out_shape=jax.ShapeDtypeStruct((M, N), out_dtype),
        compiler_params=_params("arbitrary", "arbitrary"),
        name=name,
    )(x, w, *row_tables)


def _mm_acc_kernel(x_ref, w_ref, o_ref):
    part = jnp.dot(x_ref[...], w_ref[...], preferred_element_type=F32)
    k = pl.program_id(2)

    @pl.when(k == 0)
    def _():
        o_ref[...] = part

    @pl.when(k != 0)
    def _():
        o_ref[...] += part


def _matmul_ksplit(x, w, *, bm, bn, bk, name):
    M, K = x.shape
    N = w.shape[1]
    return pl.pallas_call(
        _mm_acc_kernel,
        grid=(M // bm, N // bn, K // bk),
        in_specs=[pl.BlockSpec((bm, bk), lambda i, j, k: (i, k)),
                  pl.BlockSpec((bk, bn), lambda i, j, k: (k, j))],
        out_specs=pl.BlockSpec((bm, bn), lambda i, j, k: (i, j)),
        out_shape=jax.ShapeDtypeStruct((M, N), F32),
        compiler_params=_params("arbitrary", "arbitrary", "arbitrary"),
        name=name,
    )(x, w)


def _retention_kernel(rawf_ref, rawb_ref, q_ref, k_ref, v_ref, g_ref, o_ref,
                      state_ref, stash_ref, dmat_ref, qdec_ref, kdec_ref, cdec_ref,
                      *, chunks_per_step, n_steps):
    C = RET_CHUNK
    T = chunks_per_step
    phase = pl.program_id(2)
    step = pl.program_id(3)
    dk = q_ref.shape[1]
    dv = v_ref.shape[1]

    @pl.when((phase == 0) & (step == 0))
    def _():
        lgf = jnp.broadcast_to(-jnp.exp(rawf_ref[0])[0:1, :], (C, C))
        lgb = jnp.broadcast_to(-jnp.exp(rawb_ref[0])[0:1, :], (C, C))
        row = lax.broadcasted_iota(jnp.int32, (C, C), 0).astype(F32)
        col = lax.broadcasted_iota(jnp.int32, (C, C), 1).astype(F32)
        rel = row - col
        d_f = jnp.where(rel >= 0, jnp.exp(jnp.where(rel >= 0, rel, 0.0) * lgf), 0.0)
        d_b = jnp.where(rel < 0, jnp.exp(jnp.where(rel < 0, -rel, 0.0) * lgb), 0.0)
        dmat_ref[...] = d_f + d_b
        reps_k = dk // C
        qdec_ref[0] = jnp.concatenate([jnp.exp((row + 1.0) * lgf)] * reps_k, axis=1)
        kdec_ref[0] = jnp.concatenate([jnp.exp((C - 1.0 - row) * lgf)] * reps_k, axis=1)
        qdec_ref[1] = jnp.concatenate([jnp.exp((C - row) * lgb)] * reps_k, axis=1)
        kdec_ref[1] = jnp.concatenate([jnp.exp(row * lgb)] * reps_k, axis=1)
        reps_v = dv // C
        cdec_ref[0] = jnp.concatenate([jnp.exp(C * lgf[0:8, :])] * reps_v, axis=1)
        cdec_ref[1] = jnp.concatenate([jnp.exp(C * lgb[0:8, :])] * reps_v, axis=1)

    @pl.when(step == 0)
    def _():
        state_ref[...] = jnp.zeros_like(state_ref)

    def cross_and_update(direction, rows):
        qc = q_ref[rows, :]
        kc = k_ref[rows, :]
        vc = v_ref[rows, :]
        st = state_ref[...]
        qd = (qc.astype(F32) * qdec_ref[direction]).astype(BF16)
        cross = jnp.dot(qd, st.astype(BF16), preferred_element_type=F32)
        kd = (kc.astype(F32) * kdec_ref[direction]).astype(BF16)
        upd = lax.dot_general(kd, vc, (((0,), (0,)), ((), ())), preferred_element_type=F32)
        state_ref[...] = st * cdec_ref[direction][0:1, :] + upd
        return qc, kc, vc, cross

    @pl.when(phase == 0)
    def _():
        for c in range(T):
            rows = pl.ds(c * C, C)
            qc, kc, vc, cross = cross_and_update(0, rows)
            a = lax.dot_general(qc, kc, (((1,), (1,)), ((), ())), preferred_element_type=F32)
            pm = (a * dmat_ref[...]).astype(BF16)
            intra = jnp.dot(pm, vc, preferred_element_type=F32)
            stash_ref[step * T + c] = intra + cross

    @pl.when(phase == 1)
    def _():
        blk = n_steps - 1 - step
        for c in range(T - 1, -1, -1):
            rows = pl.ds(c * C, C)
            _, _, _, cross = cross_and_update(1, rows)
            o = stash_ref[blk * T + c] + cross
            y = _rms(o, None, NORM_EPS)
            y = y * _silu(g_ref[rows, :].astype(F32))
            o_ref[rows, :] = y.astype(o_ref.dtype)


def _retention(qk, vg, raw_f, raw_b, *, batch, n_chunks, chunks_per_step):
    H = RET_HEADS
    R = qk.shape[0]
    dk = qk.shape[1] // (2 * H)
    dv = vg.shape[1] // (2 * H)
    T = chunks_per_step
    NS = n_chunks // T
    rows = T * RET_CHUNK

    def sweep_block(b, p, s):
        return b * NS + jnp.where(p == 0, s, NS - 1 - s)

    def store_block(b, p, s):
        return b * NS + jnp.where(p == 0, NS - 1, NS - 1 - s)

    kern = functools.partial(_retention_kernel, chunks_per_step=T, n_steps=NS)
    return pl.pallas_call(
        kern,
        grid=(batch, H, 2, NS),
        in_specs=[
            pl.BlockSpec((1, 8, 128), lambda b, h, p, s: (h, 0, 0)),
            pl.BlockSpec((1, 8, 128), lambda b, h, p, s: (h, 0, 0)),
            pl.BlockSpec((rows, dk), lambda b, h, p, s: (sweep_block(b, p, s), h)),
            pl.BlockSpec((rows, dk), lambda b, h, p, s: (sweep_block(b, p, s), H + h)),
            pl.BlockSpec((rows, dv), lambda b, h, p, s: (sweep_block(b, p, s), h)),
            pl.BlockSpec((rows, dv), lambda b, h, p, s: (store_block(b, p, s), H + h)),
        ],
        out_specs=pl.BlockSpec((rows, dv), lambda b, h, p, s: (store_block(b, p, s), h)),
        out_shape=jax.ShapeDtypeStruct((R, H * dv), BF16),
        scratch_shapes=[
            pltpu.VMEM((dk, dv), F32),
            pltpu.VMEM((n_chunks, RET_CHUNK, dv), F32),
            pltpu.VMEM((RET_CHUNK, RET_CHUNK), F32),
            pltpu.VMEM((2, RET_CHUNK, dk), F32),
            pltpu.VMEM((2, RET_CHUNK, dk), F32),
            pltpu.VMEM((2, 8, dv), F32),
        ],
        compiler_params=_params("arbitrary", "arbitrary", "arbitrary", "arbitrary"),
        name="retention",
    )(raw_f, raw_b, qk, qk, vg, vg)


def _resnorm_kernel(m_ref, x_ref, post_ref, pre_ref, xo_ref, h_ref):
    xn = x_ref[...] + _rms(m_ref[...], post_ref[...], NORM_EPS)
    xo_ref[...] = xn
    h_ref[...] = _rms(xn, pre_ref[...], NORM_EPS).astype(BF16)


def _resnorm(m, x, post, pre, *, bm):
    R, D = m.shape
    row = pl.BlockSpec((bm, D), lambda i: (i, 0))
    vec = pl.BlockSpec((1, D), lambda i: (0, 0))
    return pl.pallas_call(
        _resnorm_kernel,
        grid=(R // bm,),
        in_specs=[row, row, vec, vec],
        out_specs=[row, row],
        out_shape=[jax.ShapeDtypeStruct((R, D), F32), jax.ShapeDtypeStruct((R, D), BF16)],
        compiler_params=_params("arbitrary"),
        name="resnorm",
    )(m, x, post, pre)


def _resnorm_final_kernel(m_ref, x_ref, post_ref, y_ref):
    y_ref[...] = x_ref[...] + _rms(m_ref[...], post_ref[...], NORM_EPS)


def _resnorm_final(m, x, post, *, batch):
    R, D = m.shape
    Lp = R // batch
    n_blk = Lp // ROW_ALIGN
    m3 = m.reshape(batch, Lp, D)
    x3 = x.reshape(batch, Lp, D)
    blk = (1, ROW_ALIGN, D)
    return pl.pallas_call(
        _resnorm_final_kernel,
        grid=(batch, n_blk - 1),
        in_specs=[pl.BlockSpec(blk, lambda b, t: (b, t + 1, 0)),
                  pl.BlockSpec(blk, lambda b, t: (b, t + 1, 0)),
                  pl.BlockSpec((1, 1, D), lambda b, t: (0, 0, 0))],
        out_specs=pl.BlockSpec(blk, lambda b, t: (b, t, 0)),
        out_shape=jax.ShapeDtypeStruct((batch, Lp - ROW_ALIGN, D), F32),
        compiler_params=_params("arbitrary", "arbitrary"),
        name="resnorm_final",
    )(m3, x3, post.reshape(1, 1, D))


def _diff_attn_kernel(lq1_ref, lk1_ref, lq2_ref, lk2_ref, subln_ref, bias_ref,
                      q_ref, k_ref, v_ref, g_ref, o_ref,
                      s0_ref, s1_ref, m_ref, l_ref, acc_ref,
                      *, lambda_init, key_rows, chunk_rows):
    dh = DIFF_DH
    lanes = ROW_ALIGN
    rc = chunk_rows
    tk = key_rows
    n_chunks = q_ref.shape[0] // rc
    n_items = (k_ref.shape[0] // tk) * n_chunks
    n_lane_tiles = tk // lanes
    s_bufs = (s0_ref, s1_ref)

    single_pass = k_ref.shape[0] == tk
    if not single_pass:
        m_ref[...] = jnp.full_like(m_ref, -jnp.inf)
        l_ref[...] = jnp.zeros_like(l_ref)
        acc_ref[...] = jnp.zeros_like(acc_ref)

    def item(t):
        if isinstance(t, int):
            j, c = divmod(t, n_chunks)
            return j, j * tk, c * rc
        j = lax.div(t, n_chunks)
        c = lax.rem(t, n_chunks)
        return j, pl.multiple_of(j * tk, lanes), pl.multiple_of(c * rc, P_STRIP_ROWS)

    def score(t, buf):
        j, k0, q0 = item(t)
        bias = bias_ref[j]
        for mi in range(2):
            qm = q_ref[pl.ds(q0, rc), pl.ds(mi * dh, dh)]
            km = k_ref[pl.ds(k0, tk), pl.ds(mi * dh, dh)]
            s = lax.dot_general(qm, km, (((1,), (1,)), ((), ())), preferred_element_type=F32)
            s_bufs[buf][mi, :, :lanes] = s[:, :lanes] + bias
            if n_lane_tiles > 1:
                s_bufs[buf][mi, :, lanes:] = s[:, lanes:]

    def reduce(t, buf):
        _, k0, start = item(t)
        rows = pl.ds(start, rc)
        v = v_ref[pl.ds(k0, tk), :]
        probs, alphas = [], []
        for mi in range(2):
            m_cur = jnp.max(s_bufs[buf][mi], axis=1, keepdims=True)
            if single_pass:
                m_new = jnp.broadcast_to(m_cur, (rc, lanes))
            else:
                m_prev = m_ref[mi, rows, :]
                m_new = jnp.maximum(m_prev, m_cur)
                alphas.append(jnp.exp2(m_prev - m_new))
                m_ref[mi, rows, :] = m_new
            p = jnp.exp2(s_bufs[buf][mi] - m_new[:, 0:1])
            row_part = p[:, :lanes]
            for ct in range(1, n_lane_tiles):
                row_part = row_part + p[:, ct * lanes:(ct + 1) * lanes]
            if single_pass:
                l_ref[mi, rows, :] = row_part
            else:
                l_ref[mi, rows, :] = alphas[mi] * l_ref[mi, rows, :] + row_part
            probs.append(p.astype(BF16))
        both = jnp.dot(jnp.concatenate(probs, axis=0), v, preferred_element_type=F32)
        for mi in range(2):
            pv = both[mi * rc:(mi + 1) * rc]
            if single_pass:
                acc_ref[mi, rows, :] = pv
            else:
                alpha = alphas[mi]
                acc_ref[mi, rows, :] = acc_ref[mi, rows, :] * jnp.concatenate([alpha, alpha], axis=1) + pv

    score(0, 0)

    def steady(t, carry):
        odd = lax.rem(t, 2) == 1

        @pl.when(jnp.logical_not(odd))
        def _():
            score(t + 1, 1)
            reduce(t, 0)

        @pl.when(odd)
        def _():
            score(t + 1, 0)
            reduce(t, 1)

        return carry

    lax.fori_loop(0, n_items - 1, steady, 0)
    reduce(n_items - 1, (n_items - 1) % 2)

    lam = (jnp.exp(jnp.sum(lq1_ref[...] * lk1_ref[...], axis=1, keepdims=True))
           - jnp.exp(jnp.sum(lq2_ref[...] * lk2_ref[...], axis=1, keepdims=True))
           + lambda_init)
    o1 = acc_ref[0] * (1.0 / jnp.sum(l_ref[0], axis=1, keepdims=True))
    o2 = acc_ref[1] * (lam / jnp.sum(l_ref[1], axis=1, keepdims=True))
    o = o1 - o2
    y = _rms(o, subln_ref[...], SUBLN_EPS) * (1.0 - lambda_init)
    y = y * _silu(g_ref[...].astype(F32))
    o_ref[...] = y.astype(o_ref.dtype)


def _attn_tiling(Lp):
    rc = _divisor_block(Lp, ATTN_CHUNK_ROWS, P_STRIP_ROWS)
    chunks = Lp // rc
    n = max(c for c in range(2, chunks + 1) if chunks % c == 0 and c * rc <= ATTN_BLOCK_Q)
    return n * rc, _divisor_block(Lp, ATTN_BLOCK_K, ROW_ALIGN), rc


def _diff_attention(qk, vg, lam_params, subln, *, batch, lambda_init):
    H = DIFF_HEADS
    R = qk.shape[0]
    Lp = R // batch
    tq, tk, rc = _attn_tiling(Lp)
    nq = Lp // tq
    nk = Lp // tk
    w = 2 * DIFF_DH
    pad_key = (jnp.arange(nk)[:, None, None] == 0) & (jnp.arange(ROW_ALIGN)[None, None, :] < PAD_ROWS)
    bias = jnp.where(pad_key, MASK_BIAS, 0.0).astype(F32)
    vec = pl.BlockSpec((1, DIFF_DH), lambda b, h, i: (0, 0))
    kern = functools.partial(_diff_attn_kernel, lambda_init=lambda_init, key_rows=tk, chunk_rows=rc)
    return pl.pallas_call(
        kern,
        grid=(batch, H, nq),
        in_specs=[
            vec, vec, vec, vec,
            pl.BlockSpec((1, DIFF_DV), lambda b, h, i: (0, 0)),
            pl.BlockSpec((nk, 1, ROW_ALIGN), lambda b, h, i: (0, 0, 0)),
            pl.BlockSpec((tq, w), lambda b, h, i: (b * nq + i, h)),
            pl.BlockSpec((Lp, w), lambda b, h, i: (b, H + h)),
            pl.BlockSpec((Lp, DIFF_DV), lambda b, h, i: (b, h)),
            pl.BlockSpec((tq, DIFF_DV), lambda b, h, i: (b * nq + i, H + h)),
        ],
        out_specs=pl.BlockSpec((tq, DIFF_DV), lambda b, h, i: (b * nq + i, h)),
        out_shape=jax.ShapeDtypeStruct((R, H * DIFF_DV), BF16),
        scratch_shapes=[
            pltpu.VMEM((2, rc, tk), F32),
            pltpu.VMEM((2, rc, tk), F32),
            pltpu.VMEM((2, tq, 128), F32),
            pltpu.VMEM((2, tq, 128), F32),
            pltpu.VMEM((2, tq, DIFF_DV), F32),
        ],
        compiler_params=_params("arbitrary", "arbitrary", "arbitrary"),
        name="diff_attention",
    )(*lam_params, subln, bias, qk, qk, vg, vg)


def _angles(Lp, rot_dim, theta):
    pos = jnp.arange(Lp, dtype=F32) - PAD_ROWS
    half = rot_dim // 2
    inv_freq = jnp.power(jnp.float32(theta), -jnp.arange(half, dtype=F32) * 2.0 / rot_dim)
    return pos[:, None] * inv_freq[None, :]


def _ret_tables(batch, Lp, dk):
    ang = _angles(Lp, dk, RET_THETA)
    tile = lambda t: jnp.tile(t, (batch, 1))
    return tile(jnp.cos(ang)), tile(jnp.sin(ang))


def _diff_tables(batch, Lp):
    ang = _angles(Lp, DIFF_ROT, ROPE_THETA)
    half = DIFF_ROT // 2
    cos, sin = jnp.cos(ang), jnp.sin(ang)
    zeros = lambda n: jnp.zeros((Lp, n), F32)
    c = jnp.concatenate([cos, cos, jnp.ones((Lp, DIFF_DH - DIFF_ROT), F32)], axis=1)
    s_up = jnp.concatenate([zeros(half), sin, zeros(DIFF_DH - DIFF_ROT)], axis=1)
    s_dn = jnp.concatenate([-sin, zeros(DIFF_DH - half)], axis=1)
    tile = lambda t: jnp.tile(t, (batch, 1))
    return tile(c), tile(s_up), tile(s_dn)


def _trunk(x, meta_block, pre_norm, post_norm, ret_w, diff_w, ret_decay, diff_small):
    B, S, D = x.shape
    assert S % ROW_ALIGN == 0
    n_chunks = S // ROW_ALIGN + 1
    Lp = n_chunks * ROW_ALIGN
    R = B * Lp
    bm = _divisor_block(R, MM_BLOCK_M, ROW_ALIGN)
    bn = MM_BLOCK_N

    x_cur, h = _embed_norm(x, meta_block, pre_norm[0][None])
    x_cur = x_cur.reshape(R, D)
    h = h.reshape(R, D)

    depth = pre_norm.shape[0]
    for i in range(depth):
        j = i // N_MIXERS
        if i % N_MIXERS == 0:
            w_qk, w_vg, w_out = ret_w[j]
            dk = D // RET_HEADS
            cos, sin = _ret_tables(B, Lp, dk)
            rot = functools.partial(_mm_rot_full_kernel, head_dim=dk,
                                    n_scaled_blocks=(RET_HEADS * dk) // bn, scale=dk ** -0.5)
            qk = _matmul(h, w_qk, bm=bm, bn=bn, out_dtype=BF16, body=rot,
                         row_tables=(cos, sin), name="ret_proj_qk")
            vg = _matmul(h, w_vg, bm=bm, bn=bn, out_dtype=BF16, name="ret_proj_vg")
            raw_f, raw_b = ret_decay[j]
            T = _divisor_block(n_chunks, RET_CHUNKS_PER_STEP, 1)
            o = _retention(qk, vg, raw_f, raw_b, batch=B, n_chunks=n_chunks, chunks_per_step=T)
        else:
            w_qk, w_vg, w_out = diff_w[j]
            lam_params, subln = diff_small[j]
            lambda_init = 0.8 - 0.6 * math.exp(-0.3 * i)
            tabs = _diff_tables(B, Lp)
            rot = functools.partial(_mm_rot_part_kernel, group=DIFF_DH, rot_half=DIFF_ROT // 2,
                                    n_scaled_blocks=(DIFF_HEADS * 2 * DIFF_DH) // bn,
                                    scale=DIFF_DH ** -0.5 * math.log2(math.e))
            qk = _matmul(h, w_qk, bm=bm, bn=bn, out_dtype=BF16, body=rot,
                         row_tables=tabs, name="diff_proj_qk")
            vg = _matmul(h, w_vg, bm=bm, bn=bn, out_dtype=BF16, name="diff_proj_vg")
            o = _diff_attention(qk, vg, lam_params, subln, batch=B, lambda_init=lambda_init)
        m = _matmul_ksplit(o, w_out, bm=bm, bn=bn, bk=o.shape[1] // 2, name="out_proj")
        if i + 1 < depth:
            x_cur, h = _resnorm(m, x_cur, post_norm[i][None], pre_norm[i + 1][None],
                                bm=_divisor_block(R, RESNORM_BLOCK_M, ROW_ALIGN))
        else:
            return _resnorm_final(m, x_cur, post_norm[i], batch=B)


def kernel(x_prompt, x_sample, meta_tokens, pre_norm, post_norm, ret_w_in, ret_w_out, ret_decay_fwd,
           ret_decay_bwd, diff_w_in, diff_w_out, diff_lambda_q1, diff_lambda_k1, diff_lambda_q2,
           diff_lambda_k2, diff_subln):
    D = x_prompt.shape[-1]
    meta_block = jnp.pad(meta_tokens.astype(F32), ((PAD_ROWS, 0), (0, 0)))

    ret_qk_cols = 2 * D
    ret_w = [(ret_w_in[j][:, :ret_qk_cols].astype(BF16), ret_w_in[j][:, ret_qk_cols:].astype(BF16),
              ret_w_out[j].astype(BF16)) for j in range(ret_w_in.shape[0])]
    diff_qk_cols = 2 * DIFF_HEADS * 2 * DIFF_DH
    diff_w = [(diff_w_in[j][:, :diff_qk_cols].astype(BF16), diff_w_in[j][:, diff_qk_cols:].astype(BF16),
               diff_w_out[j].astype(BF16)) for j in range(diff_w_in.shape[0])]
    bcast = lambda v: jnp.broadcast_to(v.astype(F32)[:, None, None], (v.shape[0], 8, 128))
    ret_decay = [(bcast(ret_decay_fwd[j]), bcast(ret_decay_bwd[j])) for j in range(ret_decay_fwd.shape[0])]
    diff_small = [((diff_lambda_q1[j][None], diff_lambda_k1[j][None], diff_lambda_q2[j][None],
                    diff_lambda_k2[j][None]), diff_subln[j][None]) for j in range(diff_subln.shape[0])]

    run = lambda x: _trunk(x, meta_block, pre_norm, post_norm, ret_w, diff_w, ret_decay, diff_small)
    return (run(x_prompt), run(x_sample))
```

```python
import functools
import math

import jax
import jax.numpy as jnp
from jax import lax
from jax.experimental import pallas as pl
from jax.experimental.pallas import tpu as pltpu

N_META = 16
N_MIXERS = 2
RET_HEADS = 16
RET_CHUNK = 128
RET_THETA = 10000.0
DIFF_HEADS = 32
DIFF_DH = 128
DIFF_DV = 2 * DIFF_DH
DIFF_ROT = DIFF_DH // 4
ROPE_THETA = 500000.0
NORM_EPS = 1e-6
SUBLN_EPS = 1e-5

ROW_ALIGN = 128
PAD_ROWS = ROW_ALIGN - N_META
P_STRIP_ROWS = 16
MASK_BIAS = -1e30
V7X_VMEM_LIMIT_BYTES = 56 * 1024 * 1024

MM_BLOCK_M = 1024
MM_BLOCK_N = 1024
RESNORM_BLOCK_M = 256
ATTN_BLOCK_Q = 2176
ATTN_BLOCK_K = 2176
ATTN_CHUNK_ROWS = 544
ATTN_PAIRED_MAX_ITEMS = 8
RET_CHUNKS_PER_STEP = 17

F32 = jnp.float32
BF16 = jnp.bfloat16


def _divisor_block(total, target, mult):
    best = None
    for cand in range(mult, min(total, target) + 1, mult):
        if total % cand == 0:
            best = cand
    assert best is not None, (total, target, mult)
    return best


def _params(*sem, flags=None):
    return pltpu.CompilerParams(dimension_semantics=sem, vmem_limit_bytes=V7X_VMEM_LIMIT_BYTES,
                                flags=flags)


def _rms(xf, gain, eps):
    y = xf * lax.rsqrt(jnp.mean(xf * xf, axis=-1, keepdims=True) + eps)
    if gain is not None:
        y = y * gain
    return y


def _silu(g):
    return g * (1.0 / (1.0 + jnp.exp(-g)))


def _embed_norm_kernel(x_ref, meta_ref, gain_ref, x0_ref, h_ref):
    t = pl.program_id(1)
    xb = jnp.where(t == 0, meta_ref[...], x_ref[0])
    x0_ref[0] = xb
    h_ref[0] = _rms(xb, gain_ref[...], NORM_EPS).astype(BF16)


def _embed_norm(x, meta_block, gain):
    B, S, D = x.shape
    n_blk = S // ROW_ALIGN + 1
    Lp = n_blk * ROW_ALIGN
    blk = (1, ROW_ALIGN, D)
    return pl.pallas_call(
        _embed_norm_kernel,
        grid=(B, n_blk),
        in_specs=[
            pl.BlockSpec(blk, lambda b, t: (b, jnp.maximum(t - 1, 0), 0)),
            pl.BlockSpec((ROW_ALIGN, D), lambda b, t: (0, 0)),
            pl.BlockSpec((1, D), lambda b, t: (0, 0)),
        ],
        out_specs=[pl.BlockSpec(blk, lambda b, t: (b, t, 0)),
                   pl.BlockSpec(blk, lambda b, t: (b, t, 0))],
        out_shape=[jax.ShapeDtypeStruct((B, Lp, D), F32),
                   jax.ShapeDtypeStruct((B, Lp, D), BF16)],
        compiler_params=_params("arbitrary", "arbitrary"),
        name="embed_norm",
    )(x, meta_block, gain)


def _mm_plain_kernel(x_ref, w_ref, o_ref):
    o_ref[...] = jnp.dot(x_ref[...], w_ref[...], preferred_element_type=F32).astype(o_ref.dtype)


def _mm_rot_full_kernel(x_ref, w_ref, cos_ref, sin_ref, o_ref, *, head_dim, n_scaled_blocks, scale):
    acc = jnp.dot(x_ref[...], w_ref[...], preferred_element_type=F32)
    sc = jnp.where(pl.program_id(1) < n_scaled_blocks, scale, 1.0).astype(F32)
    c = cos_ref[...] * sc
    s = sin_ref[...] * sc
    half = head_dim // 2
    for h0 in range(0, acc.shape[1], head_dim):
        x1 = acc[:, h0:h0 + half]
        x2 = acc[:, h0 + half:h0 + head_dim]
        o_ref[:, h0:h0 + half] = (x1 * c - x2 * s).astype(o_ref.dtype)
        o_ref[:, h0 + half:h0 + head_dim] = (x2 * c + x1 * s).astype(o_ref.dtype)


def _mm_rot_part_kernel(x_ref, w_ref, c_ref, s_up_ref, s_dn_ref, o_ref, *, group, rot_half,
                        n_scaled_blocks, scale):
    acc = jnp.dot(x_ref[...], w_ref[...], preferred_element_type=F32)
    sc = jnp.where(pl.program_id(1) < n_scaled_blocks, scale, 1.0).astype(F32)
    c = c_ref[...] * sc
    s_up = s_up_ref[...] * sc
    s_dn = s_dn_ref[...] * sc
    for g0 in range(0, acc.shape[1], group):
        xg = acc[:, g0:g0 + group]
        from_lo = pltpu.roll(xg, rot_half, 1)
        from_hi = pltpu.roll(xg, group - rot_half, 1)
        o_ref[:, g0:g0 + group] = (xg * c + from_lo * s_up + from_hi * s_dn).astype(o_ref.dtype)


def _matmul(x, w, *, bm, bn, out_dtype, body=_mm_plain_kernel, row_tables=(), name):
    M, K = x.shape
    N = w.shape[1]
    tab_specs = [pl.BlockSpec((bm, t.shape[1]), lambda i, j: (i, 0)) for t in row_tables]
    return pl.pallas_call(
        body,
        grid=(M // bm, N // bn),
        in_specs=[pl.BlockSpec((bm, K), lambda i, j: (i, 0)),
                  pl.BlockSpec((K, bn), lambda i, j: (0, j))] + tab_specs,
        out_specs=pl.BlockSpec((bm, bn), lambda i, j: (i, j)),
        out_shape=jax.ShapeDtypeStruct((M, N), out_dtype),
        compiler_params=_params("arbitrary", "arbitrary"),
        name=name,
    )(x, w, *row_tables)


def _mm_acc_kernel(x_ref, w_ref, o_ref):
    part = jnp.dot(x_ref[...], w_ref[...], preferred_element_type=F32)
    k = pl.program_id(2)

    @pl.when(k == 0)
    def _():
        o_ref[...] = part

    @pl.when(k != 0)
    def _():
        o_ref[...] += part


def _matmul_ksplit(x, w, *, bm, bn, bk, name):
    M, K = x.shape
    N = w.shape[1]
    return pl.pallas_call(
        _mm_acc_kernel,
        grid=(M // bm, N // bn, K // bk),
        in_specs=[pl.BlockSpec((bm, bk), lambda i, j, k: (i, k)),
                  pl.BlockSpec((bk, bn), lambda i, j, k: (k, j))],
        out_specs=pl.BlockSpec((bm, bn), lambda i, j, k: (i, j)),
        out_shape=jax.ShapeDtypeStruct((M, N), F32),
        compiler_params=_params("arbitrary", "arbitrary", "arbitrary"),
        name=name,
    )(x, w)


def _retention_kernel(rawf_ref, rawb_ref, q_ref, k_ref, v_ref, g_ref, o_ref,
                      state_ref, stash_ref, dmat_ref, qdec_ref, kdec_ref, cdec_ref,
                      *, chunks_per_step, n_steps):
    C = RET_CHUNK
    T = chunks_per_step
    phase = pl.program_id(2)
    step = pl.program_id(3)
    dk = q_ref.shape[1]
    dv = v_ref.shape[1]

    @pl.when((phase == 0) & (step == 0))
    def _():
        lgf = jnp.broadcast_to(-jnp.exp(rawf_ref[0])[0:1, :], (C, C))
        lgb = jnp.broadcast_to(-jnp.exp(rawb_ref[0])[0:1, :], (C, C))
        row = lax.broadcasted_iota(jnp.int32, (C, C), 0).astype(F32)
        col = lax.broadcasted_iota(jnp.int32, (C, C), 1).astype(F32)
        rel = row - col
        d_f = jnp.where(rel >= 0, jnp.exp(jnp.where(rel >= 0, rel, 0.0) * lgf), 0.0)
        d_b = jnp.where(rel < 0, jnp.exp(jnp.where(rel < 0, -rel, 0.0) * lgb), 0.0)
        dmat_ref[...] = d_f + d_b
        reps_k = dk // C
        qdec_ref[0] = jnp.concatenate([jnp.exp((row + 1.0) * lgf)] * reps_k, axis=1)
        kdec_ref[0] = jnp.concatenate([jnp.exp((C - 1.0 - row) * lgf)] * reps_k, axis=1)
        qdec_ref[1] = jnp.concatenate([jnp.exp((C - row) * lgb)] * reps_k, axis=1)
        kdec_ref[1] = jnp.concatenate([jnp.exp(row * lgb)] * reps_k, axis=1)
        reps_v = dv // C
        cdec_ref[0] = jnp.concatenate([jnp.exp(C * lgf[0:8, :])] * reps_v, axis=1)
        cdec_ref[1] = jnp.concatenate([jnp.exp(C * lgb[0:8, :])] * reps_v, axis=1)

    @pl.when(step == 0)
    def _():
        state_ref[...] = jnp.zeros_like(state_ref)

    def cross_and_update(direction, rows):
        qc = q_ref[rows, :]
        kc = k_ref[rows, :]
        vc = v_ref[rows, :]
        st = state_ref[...]
        qd = (qc.astype(F32) * qdec_ref[direction]).astype(BF16)
        cross = jnp.dot(qd, st.astype(BF16), preferred_element_type=F32)
        kd = (kc.astype(F32) * kdec_ref[direction]).astype(BF16)
        upd = lax.dot_general(kd, vc, (((0,), (0,)), ((), ())), preferred_element_type=F32)
        state_ref[...] = st * cdec_ref[direction][0:1, :] + upd
        return qc, kc, vc, cross

    @pl.when(phase == 0)
    def _():
        for c in range(T):
            rows = pl.ds(c * C, C)
            qc, kc, vc, cross = cross_and_update(0, rows)
            a = lax.dot_general(qc, kc, (((1,), (1,)), ((), ())), preferred_element_type=F32)
            pm = (a * dmat_ref[...]).astype(BF16)
            intra = jnp.dot(pm, vc, preferred_element_type=F32)
            stash_ref[step * T + c] = intra + cross

    @pl.when(phase == 1)
    def _():
        blk = n_steps - 1 - step
        for c in range(T - 1, -1, -1):
            rows = pl.ds(c * C, C)
            _, _, _, cross = cross_and_update(1, rows)
            o = stash_ref[blk * T + c] + cross
            y = _rms(o, None, NORM_EPS)
            y = y * _silu(g_ref[rows, :].astype(F32))
            o_ref[rows, :] = y.astype(o_ref.dtype)


def _retention(qk, vg, raw_f, raw_b, *, batch, n_chunks, chunks_per_step):
    H = RET_HEADS
    R = qk.shape[0]
    dk = qk.shape[1] // (2 * H)
    dv = vg.shape[1] // (2 * H)
    T = chunks_per_step
    NS = n_chunks // T
    rows = T * RET_CHUNK

    def sweep_block(b, p, s):
        return b * NS + jnp.where(p == 0, s, NS - 1 - s)

    def store_block(b, p, s):
        return b * NS + jnp.where(p == 0, NS - 1, NS - 1 - s)

    kern = functools.partial(_retention_kernel, chunks_per_step=T, n_steps=NS)
    return pl.pallas_call(
        kern,
        grid=(batch, H, 2, NS),
        in_specs=[
            pl.BlockSpec((1, 8, 128), lambda b, h, p, s: (h, 0, 0)),
            pl.BlockSpec((1, 8, 128), lambda b, h, p, s: (h, 0, 0)),
            pl.BlockSpec((rows, dk), lambda b, h, p, s: (sweep_block(b, p, s), h)),
            pl.BlockSpec((rows, dk), lambda b, h, p, s: (sweep_block(b, p, s), H + h)),
            pl.BlockSpec((rows, dv), lambda b, h, p, s: (sweep_block(b, p, s), h)),
            pl.BlockSpec((rows, dv), lambda b, h, p, s: (store_block(b, p, s), H + h)),
        ],
        out_specs=pl.BlockSpec((rows, dv), lambda b, h, p, s: (store_block(b, p, s), h)),
        out_shape=jax.ShapeDtypeStruct((R, H * dv), BF16),
        scratch_shapes=[
            pltpu.VMEM((dk, dv), F32),
            pltpu.VMEM((n_chunks, RET_CHUNK, dv), F32),
            pltpu.VMEM((RET_CHUNK, RET_CHUNK), F32),
            pltpu.VMEM((2, RET_CHUNK, dk), F32),
            pltpu.VMEM((2, RET_CHUNK, dk), F32),
            pltpu.VMEM((2, 8, dv), F32),
        ],
        compiler_params=_params("arbitrary", "arbitrary", "arbitrary", "arbitrary"),
        name="retention",
    )(raw_f, raw_b, qk, qk, vg, vg)


def _resnorm_kernel(m_ref, x_ref, post_ref, pre_ref, xo_ref, h_ref):
    xn = x_ref[...] + _rms(m_ref[...], post_ref[...], NORM_EPS)
    xo_ref[...] = xn
    h_ref[...] = _rms(xn, pre_ref[...], NORM_EPS).astype(BF16)


def _resnorm(m, x, post, pre, *, bm):
    R, D = m.shape
    row = pl.BlockSpec((bm, D), lambda i: (i, 0))
    vec = pl.BlockSpec((1, D), lambda i: (0, 0))
    return pl.pallas_call(
        _resnorm_kernel,
        grid=(R // bm,),
        in_specs=[row, row, vec, vec],
        out_specs=[row, row],
        out_shape=[jax.ShapeDtypeStruct((R, D), F32), jax.ShapeDtypeStruct((R, D), BF16)],
        compiler_params=_params("arbitrary"),
        name="resnorm",
    )(m, x, post, pre)


def _resnorm_final_kernel(m_ref, x_ref, post_ref, y_ref):
    y_ref[...] = x_ref[...] + _rms(m_ref[...], post_ref[...], NORM_EPS)


def _resnorm_final(m, x, post, *, batch):
    R, D = m.shape
    Lp = R // batch
    n_blk = Lp // ROW_ALIGN
    m3 = m.reshape(batch, Lp, D)
    x3 = x.reshape(batch, Lp, D)
    blk = (1, ROW_ALIGN, D)
    return pl.pallas_call(
        _resnorm_final_kernel,
        grid=(batch, n_blk - 1),
        in_specs=[pl.BlockSpec(blk, lambda b, t: (b, t + 1, 0)),
                  pl.BlockSpec(blk, lambda b, t: (b, t + 1, 0)),
                  pl.BlockSpec((1, 1, D), lambda b, t: (0, 0, 0))],
        out_specs=pl.BlockSpec(blk, lambda b, t: (b, t, 0)),
        out_shape=jax.ShapeDtypeStruct((batch, Lp - ROW_ALIGN, D), F32),
        compiler_params=_params("arbitrary", "arbitrary"),
        name="resnorm_final",
    )(m3, x3, post.reshape(1, 1, D))


def _diff_attn_kernel(lq1_ref, lk1_ref, lq2_ref, lk2_ref, subln_ref, bias_ref,
                      q_ref, k_ref, v_ref, g_ref, o_ref,
                      s0_ref, s1_ref, m_ref, l_ref, acc_ref,
                      *, lambda_init, key_rows, chunk_rows):
    dh = DIFF_DH
    lanes = ROW_ALIGN
    rc = chunk_rows
    tk = key_rows
    n_chunks = q_ref.shape[0] // rc
    n_items = (k_ref.shape[0] // tk) * n_chunks
    n_lane_tiles = tk // lanes
    s_bufs = (s0_ref, s1_ref)

    single_pass = k_ref.shape[0] == tk
    if not single_pass:
        m_ref[...] = jnp.full_like(m_ref, -jnp.inf)
        l_ref[...] = jnp.zeros_like(l_ref)
        acc_ref[...] = jnp.zeros_like(acc_ref)

    def item(t):
        if isinstance(t, int):
            j, c = divmod(t, n_chunks)
            return j, j * tk, c * rc
        j = lax.div(t, n_chunks)
        c = lax.rem(t, n_chunks)
        return j, pl.multiple_of(j * tk, lanes), pl.multiple_of(c * rc, P_STRIP_ROWS)

    def score(t, buf):
        j, k0, q0 = item(t)
        bias = bias_ref[j]
        for mi in range(2):
            qm = q_ref[pl.ds(q0, rc), pl.ds(mi * dh, dh)]
            km = k_ref[pl.ds(k0, tk), pl.ds(mi * dh, dh)]
            s = lax.dot_general(qm, km, (((1,), (1,)), ((), ())), preferred_element_type=F32)
            s_bufs[buf][mi, :, :lanes] = s[:, :lanes] + bias
            if n_lane_tiles > 1:
                s_bufs[buf][mi, :, lanes:] = s[:, lanes:]

    def reduce(t, buf):
        _, k0, start = item(t)
        rows = pl.ds(start, rc)
        v = v_ref[pl.ds(k0, tk), :]
        probs, alphas = [], []
        for mi in range(2):
            m_cur = jnp.max(s_bufs[buf][mi], axis=1, keepdims=True)
            if single_pass:
                m_new = jnp.broadcast_to(m_cur, (rc, lanes))
            else:
                m_prev = m_ref[mi, rows, :]
                m_new = jnp.maximum(m_prev, m_cur)
                alphas.append(jnp.exp2(m_prev - m_new))
                m_ref[mi, rows, :] = m_new
            p = jnp.exp2(s_bufs[buf][mi] - m_new[:, 0:1])
            row_part = p[:, :lanes]
            for ct in range(1, n_lane_tiles):
                row_part = row_part + p[:, ct * lanes:(ct + 1) * lanes]
            if single_pass:
                l_ref[mi, rows, :] = row_part
            else:
                l_ref[mi, rows, :] = alphas[mi] * l_ref[mi, rows, :] + row_part
            probs.append(p.astype(BF16))
        both = jnp.dot(jnp.concatenate(probs, axis=0), v, preferred_element_type=F32)
        for mi in range(2):
            pv = both[mi * rc:(mi + 1) * rc]
            if single_pass:
                acc_ref[mi, rows, :] = pv
            else:
                alpha = alphas[mi]
                acc_ref[mi, rows, :] = acc_ref[mi, rows, :] * jnp.concatenate([alpha, alpha], axis=1) + pv

    score(0, 0)

    if n_items <= ATTN_PAIRED_MAX_ITEMS:
        def steady_pair(i, carry):
            t = 2 * i
            score(t + 1, 1)
            reduce(t, 0)
            score(t + 2, 0)
            reduce(t + 1, 1)
            return carry

        n_pairs = (n_items - 1) // 2
        lax.fori_loop(0, n_pairs, steady_pair, 0)
        if 2 * n_pairs < n_items - 1:
            score(n_items - 1, 1)
            reduce(n_items - 2, 0)
    else:
        def steady(t, carry):
            odd = lax.rem(t, 2) == 1

            @pl.when(jnp.logical_not(odd))
            def _():
                score(t + 1, 1)
                reduce(t, 0)

            @pl.when(odd)
            def _():
                score(t + 1, 0)
                reduce(t, 1)

            return carry

        lax.fori_loop(0, n_items - 1, steady, 0)
    reduce(n_items - 1, (n_items - 1) % 2)

    lam = (jnp.exp(jnp.sum(lq1_ref[...] * lk1_ref[...], axis=1, keepdims=True))
           - jnp.exp(jnp.sum(lq2_ref[...] * lk2_ref[...], axis=1, keepdims=True))
           + lambda_init)
    o1 = acc_ref[0] * (1.0 / jnp.sum(l_ref[0], axis=1, keepdims=True))
    o2 = acc_ref[1] * (lam / jnp.sum(l_ref[1], axis=1, keepdims=True))
    o = o1 - o2
    y = _rms(o, subln_ref[...], SUBLN_EPS) * (1.0 - lambda_init)
    y = y * _silu(g_ref[...].astype(F32))
    o_ref[...] = y.astype(o_ref.dtype)


def _attn_tiling(Lp):
    rc = _divisor_block(Lp, ATTN_CHUNK_ROWS, P_STRIP_ROWS)
    chunks = Lp // rc
    n = max(c for c in range(2, chunks + 1) if chunks % c == 0 and c * rc <= ATTN_BLOCK_Q)
    return n * rc, _divisor_block(Lp, ATTN_BLOCK_K, ROW_ALIGN), rc


def _diff_attention(qk, vg, lam_params, subln, *, batch, lambda_init):
    H = DIFF_HEADS
    R = qk.shape[0]
    Lp = R // batch
    tq, tk, rc = _attn_tiling(Lp)
    nq = Lp // tq
    nk = Lp // tk
    w = 2 * DIFF_DH
    pad_key = (jnp.arange(nk)[:, None, None] == 0) & (jnp.arange(ROW_ALIGN)[None, None, :] < PAD_ROWS)
    bias = jnp.where(pad_key, MASK_BIAS, 0.0).astype(F32)
    vec = pl.BlockSpec((1, DIFF_DH), lambda b, h, i: (0, 0))
    kern = functools.partial(_diff_attn_kernel, lambda_init=lambda_init, key_rows=tk, chunk_rows=rc)
    return pl.pallas_call(
        kern,
        grid=(batch, H, nq),
        in_specs=[
            vec, vec, vec, vec,
            pl.BlockSpec((1, DIFF_DV), lambda b, h, i: (0, 0)),
            pl.BlockSpec((nk, 1, ROW_ALIGN), lambda b, h, i: (0, 0, 0)),
            pl.BlockSpec((tq, w), lambda b, h, i: (b * nq + i, h)),
            pl.BlockSpec((Lp, w), lambda b, h, i: (b, H + h)),
            pl.BlockSpec((Lp, DIFF_DV), lambda b, h, i: (b, h)),
            pl.BlockSpec((tq, DIFF_DV), lambda b, h, i: (b * nq + i, H + h)),
        ],
        out_specs=pl.BlockSpec((tq, DIFF_DV), lambda b, h, i: (b * nq + i, h)),
        out_shape=jax.ShapeDtypeStruct((R, H * DIFF_DV), BF16),
        scratch_shapes=[
            pltpu.VMEM((2, rc, tk), F32),
            pltpu.VMEM((2, rc, tk), F32),
            pltpu.VMEM((2, tq, 128), F32),
            pltpu.VMEM((2, tq, 128), F32),
            pltpu.VMEM((2, tq, DIFF_DV), F32),
        ],
        compiler_params=_params("arbitrary", "arbitrary", "arbitrary"),
        name="diff_attention",
    )(*lam_params, subln, bias, qk, qk, vg, vg)


def _angles(Lp, rot_dim, theta):
    pos = jnp.arange(Lp, dtype=F32) - PAD_ROWS
    half = rot_dim // 2
    inv_freq = jnp.power(jnp.float32(theta), -jnp.arange(half, dtype=F32) * 2.0 / rot_dim)
    return pos[:, None] * inv_freq[None, :]


def _ret_tables(batch, Lp, dk):
    ang = _angles(Lp, dk, RET_THETA)
    tile = lambda t: jnp.tile(t, (batch, 1))
    return tile(jnp.cos(ang)), tile(jnp.sin(ang))


def _diff_tables(batch, Lp):
    ang = _angles(Lp, DIFF_ROT, ROPE_THETA)
    half = DIFF_ROT // 2
    cos, sin = jnp.cos(ang), jnp.sin(ang)
    zeros = lambda n: jnp.zeros((Lp, n), F32)
    c = jnp.concatenate([cos, cos, jnp.ones((Lp, DIFF_DH - DIFF_ROT), F32)], axis=1)
    s_up = jnp.concatenate([zeros(half), sin, zeros(DIFF_DH - DIFF_ROT)], axis=1)
    s_dn = jnp.concatenate([-sin, zeros(DIFF_DH - half)], axis=1)
    tile = lambda t: jnp.tile(t, (batch, 1))
    return tile(c), tile(s_up), tile(s_dn)


def _trunk(x, meta_block, pre_norm, post_norm, ret_w, diff_w, ret_decay, diff_small):
    B, S, D = x.shape
    assert S % ROW_ALIGN == 0
    n_chunks = S // ROW_ALIGN + 1
    Lp = n_chunks * ROW_ALIGN
    R = B * Lp
    bm = _divisor_block(R, MM_BLOCK_M, ROW_ALIGN)
    bn = MM_BLOCK_N

    x_cur, h = _embed_norm(x, meta_block, pre_norm[0][None])
    x_cur = x_cur.reshape(R, D)
    h = h.reshape(R, D)

    depth = pre_norm.shape[0]
    for i in range(depth):
        j = i // N_MIXERS
        if i % N_MIXERS == 0:
            w_qk, w_vg, w_out = ret_w[j]
            dk = D // RET_HEADS
            cos, sin = _ret_tables(B, Lp, dk)
            rot = functools.partial(_mm_rot_full_kernel, head_dim=dk,
                                    n_scaled_blocks=(RET_HEADS * dk) // bn, scale=dk ** -0.5)
            qk = _matmul(h, w_qk, bm=bm, bn=bn, out_dtype=BF16, body=rot,
                         row_tables=(cos, sin), name="ret_proj_qk")
            vg = _matmul(h, w_vg, bm=bm, bn=bn, out_dtype=BF16, name="ret_proj_vg")
            raw_f, raw_b = ret_decay[j]
            T = _divisor_block(n_chunks, RET_CHUNKS_PER_STEP, 1)
            o = _retention(qk, vg, raw_f, raw_b, batch=B, n_chunks=n_chunks, chunks_per_step=T)
        else:
            w_qk, w_vg, w_out = diff_w[j]
            lam_params, subln = diff_small[j]
            lambda_init = 0.8 - 0.6 * math.exp(-0.3 * i)
            tabs = _diff_tables(B, Lp)
            rot = functools.partial(_mm_rot_part_kernel, group=DIFF_DH, rot_half=DIFF_ROT // 2,
                                    n_scaled_blocks=(DIFF_HEADS * 2 * DIFF_DH) // bn,
                                    scale=DIFF_DH ** -0.5 * math.log2(math.e))
            qk = _matmul(h, w_qk, bm=bm, bn=bn, out_dtype=BF16, body=rot,
                         row_tables=tabs, name="diff_proj_qk")
            vg = _matmul(h, w_vg, bm=bm, bn=bn, out_dtype=BF16, name="diff_proj_vg")
            o = _diff_attention(qk, vg, lam_params, subln, batch=B, lambda_init=lambda_init)
        m = _matmul_ksplit(o, w_out, bm=bm, bn=bn, bk=o.shape[1] // 2, name="out_proj")
        if i + 1 < depth:
            x_cur, h = _resnorm(m, x_cur, post_norm[i][None], pre_norm[i + 1][None],
                                bm=_divisor_block(R, RESNORM_BLOCK_M, ROW_ALIGN))
        else:
            return _resnorm_final(m, x_cur, post_norm[i], batch=B)


def kernel(x_prompt, x_sample, meta_tokens, pre_norm, post_norm, ret_w_in, ret_w_out, ret_decay_fwd,
           ret_decay_bwd, diff_w_in, diff_w_out, diff_lambda_q1, diff_lambda_k1, diff_lambda_q2,
           diff_lambda_k2, diff_subln):
    D = x_prompt.shape[-1]
    meta_block = jnp.pad(meta_tokens.astype(F32), ((PAD_ROWS, 0), (0, 0)))

    ret_qk_cols = 2 * D
    ret_w = [(ret_w_in[j][:, :ret_qk_cols].astype(BF16), ret_w_in[j][:, ret_qk_cols:].astype(BF16),
              ret_w_out[j].astype(BF16)) for j in range(ret_w_in.shape[0])]
    diff_qk_cols = 2 * DIFF_HEADS * 2 * DIFF_DH
    diff_w = [(diff_w_in[j][:, :diff_qk_cols].astype(BF16), diff_w_in[j][:, diff_qk_cols:].astype(BF16),
               diff_w_out[j].astype(BF16)) for j in range(diff_w_in.shape[0])]
    bcast = lambda v: jnp.broadcast_to(v.astype(F32)[:, None, None], (v.shape[0], 8, 128))
    ret_decay = [(bcast(ret_decay_fwd[j]), bcast(ret_decay_bwd[j])) for j in range(ret_decay_fwd.shape[0])]
    diff_small = [((diff_lambda_q1[j][None], diff_lambda_k1[j][None], diff_lambda_q2[j][None],
                    diff_lambda_k2[j][None]), diff_subln[j][None]) for j in range(diff_subln.shape[0])]

    run = lambda x: _trunk(x, meta_block, pre_norm, post_norm, ret_w, diff_w, ret_decay, diff_small)
    return (run(x_prompt), run(x_sample))
```
